```python
import math
import jax, jax.numpy as jnp
from jax import lax
import numpy as np

D_MODEL = 2048
BATCH = 2
SEQ = 8192
DEPTH = 4

N_META = 16
BLOCK = 128
FRONT_PAD = (-N_META) % BLOCK

RWKV_WIDTH = D_MODEL // 2
RWKV_HEAD_DIM = 64
RWKV_HEADS = RWKV_WIDTH // RWKV_HEAD_DIM
FOX_WIDTH = D_MODEL - RWKV_WIDTH
FOX_HEAD_DIM = 64
FOX_HEADS = FOX_WIDTH // FOX_HEAD_DIM


def _lora_rank(c, power, factor):
    return max(32, int(round(c ** power * factor / 32)) * 32)


DECAY_RANK = _lora_rank(RWKV_WIDTH, 0.5, 1.8)
AAA_RANK = _lora_rank(RWKV_WIDTH, 0.5, 1.8)
MV_RANK = _lora_rank(RWKV_WIDTH, 0.5, 1.3)
GATE_RANK = _lora_rank(RWKV_WIDTH, 0.8, 0.6)

D_FF = 4 * D_MODEL
ALPHA = (2 * DEPTH) ** 0.25
BETA = (8 * DEPTH) ** -0.25
LN_EPS = 1e-5
GN_EPS = 64e-5
OUT_NORM_EPS = 1e-6
DECAY_SCALE = math.exp(-0.5)
NEG_INF = -1e30

FOX_Q = 0
FOX_K = FOX_WIDTH
FOX_V = 2 * FOX_WIDTH
FOX_F = 3 * FOX_WIDTH
RWKV_START = 3 * FOX_WIDTH + FOX_HEADS
R_R = 0
R_K = RWKV_WIDTH
R_V = 2 * RWKV_WIDTH
R_W = 3 * RWKV_WIDTH
R_A = R_W + DECAY_RANK
R_G = R_A + AAA_RANK
R_VL = R_G + GATE_RANK
RWKV_COLS_FIRST = R_VL
RWKV_COLS = R_VL + MV_RANK
IN_COLS_FIRST = RWKV_START + RWKV_COLS_FIRST
IN_COLS = RWKV_START + RWKV_COLS

kernel_name = "fox_rwkv7_hymba_deepnorm_trunk"


def _layer_norm(x, g, b):
    xf = x.astype(jnp.float32)
    mean = jnp.mean(xf, -1, keepdims=True)
    var = jnp.mean(jnp.square(xf - mean), -1, keepdims=True)
    return ((xf - mean) * lax.rsqrt(var + LN_EPS) * g + b).astype(x.dtype)


def _token_shift(p):
    return jnp.pad(p, ((0, 0), (1, 0), (0, 0)))[:, :-1]


def _fox_attention(q, k, v, log_f):
    b, t, h, dh = q.shape
    c = jnp.cumsum(log_f, axis=1)
    tp = t + FRONT_PAD
    nb = tp // BLOCK

    def blocks(a):
        a = jnp.pad(a, ((0, 0), (FRONT_PAD, 0)) + ((0, 0),) * (a.ndim - 2))
        a = a.reshape((b, nb, BLOCK) + a.shape[2:])
        return jnp.swapaxes(jnp.moveaxis(a, 1, 0), 2, 3)

    qb, kb, vb, cb = blocks(q), blocks(k), blocks(v), blocks(c)
    scale = FOX_HEAD_DIM ** -0.5
    offs = jnp.arange(BLOCK)

    def one_block(args):
        i, qi, cqi = args
        qpos = i * BLOCK + offs

        def body(j, carry):
            m, l, acc = carry
            kj, vj, ckj = kb[j], vb[j], cb[j]
            kpos = j * BLOCK + offs
            s = (jnp.einsum('bhqd,bhkd->bhqk', qi, kj).astype(jnp.float32) * scale
                 + cqi[..., :, None] - ckj[..., None, :])
            valid = (kpos[None, :] <= qpos[:, None]) & (kpos[None, :] >= FRONT_PAD)
            s = jnp.where(valid, s, NEG_INF)
            m_new = jnp.maximum(m, jnp.max(s, -1))
            corr = jnp.exp(m - m_new)
            p = jnp.exp(s - m_new[..., None])
            l = l * corr + jnp.sum(p, -1)
            acc = acc * corr[..., None] + jnp.einsum('bhqk,bhkd->bhqd', p, vj.astype(jnp.float32))
            return m_new, l, acc

        init = (jnp.full((b, h, BLOCK), NEG_INF, jnp.float32),
                jnp.zeros((b, h, BLOCK), jnp.float32),
                jnp.zeros((b, h, BLOCK, dh), jnp.float32))
        _, l, acc = lax.fori_loop(0, i + 1, body, init)
        return (acc / l[..., None]).astype(v.dtype)

    out = lax.map(one_block, (jnp.arange(nb), qb, cb))
    out = jnp.moveaxis(out, 0, 2).reshape(b, h, tp, dh)
    return jnp.swapaxes(out, 1, 2)[:, FRONT_PAD:]


def _rwkv7_mixer(p, w_up, w0, a_up, a0, g_up, v_up, v0, k_k, k_a, r_k, lnx_g, lnx_b, v_first):
    b, t, _ = p.shape
    r = p[..., R_R:R_K]
    k = p[..., R_K:R_V]
    v = p[..., R_V:R_W]
    decay = jnp.exp(-DECAY_SCALE * jax.nn.sigmoid(
        (w0 + jnp.tanh(p[..., R_W:R_A]) @ w_up).astype(jnp.float32)))
    a = jax.nn.sigmoid(a0 + p[..., R_A:R_G] @ a_up)
    g = jax.nn.sigmoid(p[..., R_G:R_VL]) @ g_up
    if v_up is None:
        v_first = v
    else:
        v = v + (v_first - v) * jax.nn.sigmoid(v0 + p[..., R_VL:] @ v_up)

    def heads(z):
        return z.reshape(b, t, RWKV_HEADS, RWKV_HEAD_DIM).astype(jnp.float32)

    kk = heads(k * k_k)
    kk = kk / jnp.maximum(jnp.sqrt(jnp.sum(kk * kk, -1, keepdims=True)), 1e-12)
    k = k * (1 + (a - 1) * k_a)
    xs = tuple(jnp.moveaxis(z, 1, 0) for z in
               (heads(r), heads(decay), heads(k), heads(v), kk, heads(a)))

    def step(state, inp):
        r_t, w_t, k_t, v_t, kk_t, a_t = inp
        sa = jnp.einsum('bhij,bhj->bhi', state, -kk_t)
        state = (state * w_t[:, :, None, :] + sa[..., None] * (kk_t * a_t)[:, :, None, :]
                 + v_t[..., None] * k_t[:, :, None, :])
        return state, jnp.einsum('bhij,bhj->bhi', state, r_t)

    s0 = jnp.zeros((b, RWKV_HEADS, RWKV_HEAD_DIM, RWKV_HEAD_DIM), jnp.float32)
    _, y = lax.scan(step, s0, xs)
    y = jnp.moveaxis(y, 0, 1)
    mean = jnp.mean(y, -1, keepdims=True)
    var = jnp.mean(jnp.square(y - mean), -1, keepdims=True)
    y = ((y - mean) * lax.rsqrt(var + GN_EPS)).reshape(b, t, RWKV_WIDTH) * lnx_g + lnx_b
    bonus = (jnp.sum(heads(r * k * r_k), -1, keepdims=True) * heads(v)).reshape(b, t, RWKV_WIDTH)
    return ((y + bonus) * g).astype(p.dtype), v_first


def _hybrid_layer(x, w_in, mu, fox_fb, fox_out_g, w_up, w0, a_up, a0, g_up, v_up, v0,
                  k_k, k_a, r_k, lnx_g, lnx_b, w_out, ln1_g, ln1_b, w_ff1, w_ff2,
                  ln2_g, ln2_b, v_first):
    b, t, _ = x.shape
    h = x @ w_in
    fh = lambda z: z.reshape(b, t, FOX_HEADS, FOX_HEAD_DIM)
    log_f = jax.nn.log_sigmoid((h[..., FOX_F:RWKV_START] + fox_fb).astype(jnp.float32))
    o = _fox_attention(fh(h[..., FOX_Q:FOX_K]), fh(h[..., FOX_K:FOX_V]),
                       fh(h[..., FOX_V:FOX_F]), log_f).astype(jnp.float32)
    o = o * lax.rsqrt(jnp.mean(o * o, -1, keepdims=True) + OUT_NORM_EPS) * fox_out_g
    fox_out = o.reshape(b, t, FOX_WIDTH).astype(x.dtype)
    p = h[..., RWKV_START:]
    p = p + mu * (_token_shift(p) - p)
    rwkv_out, v_first = _rwkv7_mixer(p, w_up, w0, a_up, a0, g_up, v_up, v0,
                                     k_k, k_a, r_k, lnx_g, lnx_b, v_first)
    mixed = jnp.concatenate([fox_out, rwkv_out], axis=-1) @ w_out
    x = _layer_norm(ALPHA * x + mixed, ln1_g, ln1_b)
    ff = jnp.square(jax.nn.relu(x @ w_ff1)) @ w_ff2
    x = _layer_norm(ALPHA * x + ff, ln2_g, ln2_b)
    return x, v_first


def setup_inputs(seed: int = 0) -> dict:
    key = jax.random.key(seed)
    ks = jax.random.split(key, 32)
    f32 = jnp.float32
    nrm = lambda k, shape, s: jax.random.normal(k, shape, f32) * s
    gain = lambda k, shape: 1.0 + 0.02 * jax.random.normal(k, shape, f32)
    uni = lambda k, shape, lo, hi: jax.random.uniform(k, shape, f32, lo, hi)
    l1 = DEPTH - 1
    return {
        "x": nrm(ks[0], (BATCH, SEQ, D_MODEL), 1.0),
        "meta": nrm(ks[1], (N_META, D_MODEL), 1.0),
        "ln_in_g": gain(ks[2], (D_MODEL,)),
        "ln_in_b": nrm(ks[3], (D_MODEL,), 0.02),
        "w_in_first": nrm(ks[4], (D_MODEL, IN_COLS_FIRST), D_MODEL ** -0.5),
        "w_in_rest": nrm(ks[5], (l1, D_MODEL, IN_COLS), D_MODEL ** -0.5),
        "mu_first": uni(ks[6], (RWKV_COLS_FIRST,), 0.0, 1.0),
        "mu_rest": uni(ks[7], (l1, RWKV_COLS), 0.0, 1.0),
        "fox_fb": uni(ks[8], (DEPTH, FOX_HEADS), 1.0, 4.0),
        "fox_out_g": gain(ks[9], (DEPTH, FOX_HEADS, FOX_HEAD_DIM)),
        "w_up": nrm(ks[10], (DEPTH, DECAY_RANK, RWKV_WIDTH), 0.1 * DECAY_RANK ** -0.5),
        "w0": uni(ks[11], (DEPTH, RWKV_WIDTH), -5.0, 0.0),
        "a_up": nrm(ks[12], (DEPTH, AAA_RANK, RWKV_WIDTH), 0.1 * AAA_RANK ** -0.5),
        "a0": nrm(ks[13], (DEPTH, RWKV_WIDTH), 0.1),
        "g_up": nrm(ks[14], (DEPTH, GATE_RANK, RWKV_WIDTH), GATE_RANK ** -0.5),
        "v_up": nrm(ks[15], (l1, MV_RANK, RWKV_WIDTH), 0.1 * MV_RANK ** -0.5),
        "v0": nrm(ks[16], (l1, RWKV_WIDTH), 0.1),
        "k_k": 0.85 + nrm(ks[17], (DEPTH, RWKV_WIDTH), 0.02),
        "k_a": gain(ks[18], (DEPTH, RWKV_WIDTH)),
        "r_k": nrm(ks[19], (DEPTH, RWKV_WIDTH), 0.1),
        "lnx_g": gain(ks[20], (DEPTH, RWKV_WIDTH)),
        "lnx_b": nrm(ks[21], (DEPTH, RWKV_WIDTH), 0.02),
        "w_out": nrm(ks[22], (DEPTH, D_MODEL, D_MODEL), BETA * D_MODEL ** -0.5),
        "ln1_g": gain(ks[23], (DEPTH, D_MODEL)),
        "ln1_b": nrm(ks[24], (DEPTH, D_MODEL), 0.02),
        "w_ff1": nrm(ks[25], (DEPTH, D_MODEL, D_FF), D_MODEL ** -0.5),
        "w_ff2": nrm(ks[26], (DEPTH, D_FF, D_MODEL), BETA * D_FF ** -0.5),
        "ln2_g": gain(ks[27], (DEPTH, D_MODEL)),
        "ln2_b": nrm(ks[28], (DEPTH, D_MODEL), 0.02),
    }


def reference(x, meta, ln_in_g, ln_in_b, w_in_first, w_in_rest, mu_first, mu_rest,
              fox_fb, fox_out_g, w_up, w0, a_up, a0, g_up, v_up, v0, k_k, k_a, r_k,
              lnx_g, lnx_b, w_out, ln1_g, ln1_b, w_ff1, w_ff2, ln2_g, ln2_b):
    b = x.shape[0]
    h = jnp.concatenate(
        [jnp.broadcast_to(meta.astype(x.dtype)[None], (b, N_META, D_MODEL)), x], axis=1)
    h = _layer_norm(h, ln_in_g, ln_in_b)
    v_first = None
    for l in range(DEPTH):
        first = l == 0
        h, v_first = _hybrid_layer(
            h,
            w_in_first if first else w_in_rest[l - 1],
            mu_first if first else mu_rest[l - 1],
            fox_fb[l], fox_out_g[l], w_up[l], w0[l], a_up[l], a0[l], g_up[l],
            None if first else v_up[l - 1],
            None if first else v0[l - 1],
            k_k[l], k_a[l], r_k[l], lnx_g[l], lnx_b[l], w_out[l],
            ln1_g[l], ln1_b[l], w_ff1[l], w_ff2[l], ln2_g[l], ln2_b[l], v_first)
    return h[:, N_META:]
```

```python
import functools
import math

import jax
import jax.numpy as jnp
from jax import lax
from jax.experimental import pallas as pl
from jax.experimental.pallas import tpu as pltpu

F32 = jnp.float32
BF16 = jnp.bfloat16
HIGHEST = lax.Precision.HIGHEST

D_MODEL = 2048
DEPTH = 4
N_META = 16
LANES = 128
FRONT_PAD = (-N_META) % LANES
HEAD_DIM = 64
WIDTH = D_MODEL // 2
HEADS = WIDTH // HEAD_DIM
DECAY_RANK = 64
AAA_RANK = 64
GATE_RANK = 160
MV_RANK = 32
D_FF = 4 * D_MODEL
ALPHA = (2 * DEPTH) ** 0.25
LN_EPS = 1e-5
GN_EPS = 64e-5
OUT_NORM_EPS = 1e-6
DECAY_SCALE = math.exp(-0.5)
NEG_INF = -1e30
MASKED_KEY = 1e30

SM_F, SM_W, SM_A, SM_G, SM_V, SM_COLS = 0, 128, 256, 384, 640, 768

CHUNK = 64
SCAN_HEADS = 4
SCAN_W = SCAN_HEADS * HEAD_DIM
VMEM_LIMIT = 56 * 1024 * 1024


def _tile(total, cap, mult=64):
    best = None
    for d in range(mult, cap + 1, mult):
        if total % d == 0:
            best = d
    assert best is not None, (total, cap)
    return best


def _params(sem):
    return pltpu.CompilerParams(dimension_semantics=sem, vmem_limit_bytes=VMEM_LIMIT)


def _layer_norm(x, g, b):
    mean = jnp.mean(x, -1, keepdims=True)
    xc = x - mean
    var = jnp.mean(xc * xc, -1, keepdims=True)
    return xc * lax.rsqrt(var + LN_EPS) * g + b


def _ln_kernel(x_ref, g_ref, b_ref, of_ref, ob_ref):
    y = _layer_norm(x_ref[...], g_ref[...], b_ref[...])
    of_ref[...] = y
    ob_ref[...] = y.astype(BF16)


def _ln_rows(x, g, b, tp):
    n, d = x.shape
    tm = _tile(tp, 320)
    row = pl.BlockSpec((tm, d), lambda i: (i, 0))
    vec = pl.BlockSpec((1, d), lambda i: (0, 0))
    return pl.pallas_call(
        _ln_kernel,
        grid=(n // tm,),
        in_specs=[row, vec, vec],
        out_specs=[row, row],
        out_shape=[jax.ShapeDtypeStruct((n, d), F32), jax.ShapeDtypeStruct((n, d), BF16)],
        compiler_params=_params(("parallel",)),
        name="ln_in",
    )(x, g.reshape(1, d), b.reshape(1, d))


def _mm_kernel(x_ref, w_ref, o_ref):
    o_ref[...] = jnp.dot(x_ref[...], w_ref[...], preferred_element_type=F32).astype(o_ref.dtype)


def _matmul(x, w, out_dtype, tp, name):
    n, k = x.shape
    nc = w.shape[1]
    tm = _tile(tp, 640)
    tn = _tile(nc, 768, LANES)
    return pl.pallas_call(
        _mm_kernel,
        grid=(n // tm, nc // tn),
        in_specs=[pl.BlockSpec((tm, k), lambda i, j: (i, 0)),
                  pl.BlockSpec((k, tn), lambda i, j: (0, j))],
        out_specs=pl.BlockSpec((tm, tn), lambda i, j: (i, j)),
        out_shape=jax.ShapeDtypeStruct((n, nc), out_dtype),
        compiler_params=_params(("parallel", "parallel")),
        name=name,
    )(x, w)


def _cumsum_kernel(f_ref, fb_ref, o_ref, carry_ref):
    t = pl.program_id(1)

    @pl.when(t == 0)
    def _():
        carry_ref[...] = jnp.zeros_like(carry_ref)

    z = f_ref[...] + fb_ref[...]
    lf = jnp.minimum(z, 0.0) - jnp.log1p(jnp.exp(-jnp.abs(z)))
    ri = lax.broadcasted_iota(jnp.int32, (LANES, LANES), 0)
    ci = lax.broadcasted_iota(jnp.int32, (LANES, LANES), 1)
    real = (t * LANES + ri) >= FRONT_PAD
    lf = jnp.where(real, lf, 0.0)
    tri = (ci <= ri).astype(F32)
    c = jnp.dot(tri, lf, precision=HIGHEST, preferred_element_type=F32) + carry_ref[...]
    carry_ref[...] = c[LANES - 1:LANES, :]
    c = jnp.where(real, c, MASKED_KEY)
    o_ref[0] = c.T[0:HEADS, :]


def _fox_cumsum(small, fb_pad, b, tp):
    nblk = tp // LANES
    return pl.pallas_call(
        _cumsum_kernel,
        grid=(b, nblk),
        in_specs=[pl.BlockSpec((LANES, LANES), lambda bi, t: (bi * nblk + t, 0)),
                  pl.BlockSpec((1, LANES), lambda bi, t: (0, 0))],
        out_specs=pl.BlockSpec((1, HEADS, LANES), lambda bi, t: (bi, 0, t)),
        out_shape=jax.ShapeDtypeStruct((b, HEADS, tp), F32),
        scratch_shapes=[pltpu.VMEM((1, LANES), F32)],
        compiler_params=_params(("parallel", "arbitrary")),
        name="fox_cumsum",
    )(small, fb_pad)


def _fox_kernel(q_ref, k_ref, v_ref, c_ref, g_ref, o_ref, m_sc, l_sc, acc_sc, *, blk):
    qi = pl.program_id(2)
    q2 = q_ref[...]
    qh = [q2[:, HEAD_DIM * h:HEAD_DIM * (h + 1)] for h in range(2)]
    ri = lax.broadcasted_iota(jnp.int32, (blk, blk), 0)
    ci = lax.broadcasted_iota(jnp.int32, (blk, blk), 1)
    causal = ci <= ri
    for h in range(2):
        m_sc[h] = jnp.full((blk, 1), NEG_INF, F32)
        l_sc[h] = jnp.zeros((blk, 1), F32)
        acc_sc[h] = jnp.zeros((blk, HEAD_DIM), F32)

    def step(j, masked):
        off = pl.multiple_of(j * blk, LANES)
        k2 = k_ref[pl.ds(off, blk), :]
        v2 = v_ref[pl.ds(off, blk), :]
        for h in range(2):
            kh = k2[:, HEAD_DIM * h:HEAD_DIM * (h + 1)]
            vh = v2[:, HEAD_DIM * h:HEAD_DIM * (h + 1)]
            s = lax.dot_general(qh[h], kh, (((1,), (1,)), ((), ())), preferred_element_type=F32)
            s = s - c_ref[0, 0, h:h + 1, pl.ds(off, blk)]
            if masked:
                s = jnp.where(causal, s, NEG_INF)
            m_old = m_sc[h]
            m_new = jnp.maximum(m_old, jnp.max(s, axis=1, keepdims=True))
            p = jnp.exp(s - m_new)
            corr = jnp.exp(m_old - m_new)
            l_sc[h] = corr * l_sc[h] + jnp.sum(p, axis=1, keepdims=True)
            acc_sc[h] = corr * acc_sc[h] + jnp.dot(p.astype(BF16), vh, preferred_element_type=F32)
            m_sc[h] = m_new

    def body(j, carry):
        step(j, False)
        return carry

    lax.fori_loop(0, qi, body, 0)
    step(qi, True)

    outs = []
    for h in range(2):
        o = acc_sc[h] / l_sc[h]
        o = o * lax.rsqrt(jnp.mean(o * o, -1, keepdims=True) + OUT_NORM_EPS)
        outs.append(o)
    o_ref[...] = (jnp.concatenate(outs, axis=1) * g_ref[0]).astype(o_ref.dtype)


def _fox_attention(qkv, c4, gains, b, tp):
    blk = _tile(tp, 640, LANES)
    nq = tp // blk
    hp = HEADS // 2
    kern = functools.partial(_fox_kernel, blk=blk)
    return pl.pallas_call(
        kern,
        grid=(b, hp, nq),
        in_specs=[pl.BlockSpec((blk, LANES), lambda bi, h, qi: (bi * nq + qi, h)),
                  pl.BlockSpec((tp, LANES), lambda bi, h, qi: (bi, hp + h)),
                  pl.BlockSpec((tp, LANES), lambda bi, h, qi: (bi, 2 * hp + h)),
                  pl.BlockSpec((1, 1, 2, tp), lambda bi, h, qi: (bi, h, 0, 0)),
                  pl.BlockSpec((1, 1, LANES), lambda bi, h, qi: (h, 0, 0))],
        out_specs=pl.BlockSpec((blk, LANES), lambda bi, h, qi: (bi * nq + qi, h)),
        out_shape=jax.ShapeDtypeStruct((b * tp, WIDTH), BF16),
        scratch_shapes=[pltpu.VMEM((2, blk, 1), F32), pltpu.VMEM((2, blk, 1), F32),
                        pltpu.VMEM((2, blk, HEAD_DIM), F32)],
        compiler_params=_params(("parallel", "parallel", "arbitrary")),
        name="fox_attention",
    )(qkv, qkv, qkv, c4, gains)


def _head_sum(x, sel_ref, selt_ref):
    s = jnp.dot(x, sel_ref[...], precision=HIGHEST, preferred_element_type=F32)
    return jnp.dot(s, selt_ref[...], precision=HIGHEST, preferred_element_type=F32)


def _prep_kernel(*refs, tm, tiles_per_batch, first):
    if first:
        (big_ref, sm_ref, mub_ref, mus_ref, wup_ref, w0_ref, aup_ref, a0_ref, gup_ref,
         kk_ref, ka_ref, rk_ref, sel_ref, selt_ref,
         r_o, lw_o, k_o, v_o, kn_o, a_o, g_o, bg_o, vf_o, cb_sc, cs_sc) = refs
    else:
        (big_ref, sm_ref, mub_ref, mus_ref, wup_ref, w0_ref, aup_ref, a0_ref, gup_ref,
         vup_ref, v0_ref, vf_ref, kk_ref, ka_ref, rk_ref, sel_ref, selt_ref,
         r_o, lw_o, k_o, v_o, kn_o, a_o, g_o, bg_o, cb_sc, cs_sc) = refs
    i = pl.program_id(0)

    @pl.when(i == 0)
    def _():
        cb_sc[...] = jnp.zeros_like(cb_sc)
        cs_sc[...] = jnp.zeros_like(cs_sc)

    ri = lax.broadcasted_iota(jnp.int32, (tm, 1), 0)
    real = (lax.rem(i, tiles_per_batch) * tm + ri) >= FRONT_PAD

    def mix(x_ref, mu_ref, carry):
        h = jnp.where(real, x_ref[...], 0.0)
        prev = jnp.where(ri == 0, carry[...], pltpu.roll(h, 1, axis=0))
        carry[...] = h[tm - 1:tm, :]
        p = h + mu_ref[...] * (prev - h)
        return jnp.where(real, p, 0.0)

    pb = mix(big_ref, mub_ref, cb_sc)
    ps = mix(sm_ref, mus_ref, cs_sc)
    r = pb[:, 0:WIDTH]
    k = pb[:, WIDTH:2 * WIDTH]
    v = pb[:, 2 * WIDTH:3 * WIDTH]

    def lora(x, w_ref):
        return jnp.dot(x.astype(BF16), w_ref[...], preferred_element_type=F32)

    lw = -DECAY_SCALE * jax.nn.sigmoid(w0_ref[...] + lora(jnp.tanh(ps[:, SM_W:SM_A]), wup_ref))
    a = jax.nn.sigmoid(a0_ref[...] + lora(ps[:, SM_A:SM_G], aup_ref))
    g = lora(jax.nn.sigmoid(ps[:, SM_G:SM_V]), gup_ref)
    if first:
        vf_o[...] = v
    else:
        v = v + (vf_ref[...] - v) * jax.nn.sigmoid(v0_ref[...] + lora(ps[:, SM_V:SM_COLS], vup_ref))

    kn = k * kk_ref[...]
    ss = jnp.dot(kn * kn, sel_ref[...], precision=HIGHEST, preferred_element_type=F32)
    inv = 1.0 / jnp.maximum(jnp.sqrt(ss), 1e-12)
    kn = kn * jnp.dot(inv, selt_ref[...], precision=HIGHEST, preferred_element_type=F32)
    k = k * (1.0 + (a - 1.0) * ka_ref[...])
    bonus = _head_sum(r * k * rk_ref[...], sel_ref, selt_ref) * v

    r_o[...] = r
    lw_o[...] = lw
    k_o[...] = k
    v_o[...] = v
    kn_o[...] = kn
    a_o[...] = a
    g_o[...] = g
    bg_o[...] = bonus * g


def _rwkv_prep(big, small, lp, sel, selt, v_first, tp):
    n = big.shape[0]
    first = v_first is None
    tm = _tile(tp, 128)
    row = lambda c: pl.BlockSpec((tm, c), lambda i: (i, 0))
    full = lambda a: pl.BlockSpec(a.shape, lambda i: (0,) * a.ndim)
    ins = [big, small, lp["mu_big"], lp["mu_small"], lp["w_up"], lp["w0"], lp["a_up"], lp["a0"], lp["g_up"]]
    specs = [row(3 * WIDTH), row(SM_COLS)] + [full(a) for a in ins[2:]]
    if not first:
        ins += [lp["v_up"], lp["v0"], v_first]
        specs += [full(lp["v_up"]), full(lp["v0"]), row(WIDTH)]
    tail = [lp["k_k"], lp["k_a"], lp["r_k"], sel, selt]
    ins += tail
    specs += [full(a) for a in tail]
    n_out = 9 if first else 8
    kern = functools.partial(_prep_kernel, tm=tm, tiles_per_batch=tp // tm, first=first)
    return pl.pallas_call(
        kern,
        grid=(n // tm,),
        in_specs=specs,
        out_specs=[row(WIDTH)] * n_out,
        out_shape=[jax.ShapeDtypeStruct((n, WIDTH), F32)] * n_out,
        scratch_shapes=[pltpu.VMEM((1, 3 * WIDTH), F32), pltpu.VMEM((1, SM_COLS), F32)],
        compiler_params=_params(("arbitrary",)),
        name="rwkv_prep",
    )(*ins)


def _scan_kernel(r_ref, lw_ref, k_ref, v_ref, kn_ref, a_ref, y_ref, s_sc, *, tt):
    ti = pl.program_id(2)

    @pl.when(ti == 0)
    def _():
        s_sc[...] = jnp.zeros_like(s_sc)

    w = SCAN_W
    ri = lax.broadcasted_iota(jnp.int32, (w, w), 0)
    ci = lax.broadcasted_iota(jnp.int32, (w, w), 1)
    same_head = (ri // HEAD_DIM) == (ci // HEAD_DIM)
    strict = same_head & (ci < ri)
    incl = same_head & (ci <= ri)
    eye = (ri == ci).astype(F32)
    tri_r = lax.broadcasted_iota(jnp.int32, (CHUNK, CHUNK), 0)
    tri_c = lax.broadcasted_iota(jnp.int32, (CHUNK, CHUNK), 1)
    tri = (tri_c <= tri_r).astype(F32)

    def stack(x):
        return jnp.where(same_head, jnp.tile(x, (SCAN_HEADS, 1)), 0.0).astype(BF16)

    def mm(x, y):
        return jnp.dot(x, y, preferred_element_type=F32)

    def mm_nt(x, y):
        return lax.dot_general(x, y, (((1,), (1,)), ((), ())), preferred_element_type=F32)

    def mm_tn(x, y):
        return lax.dot_general(x, y, (((0,), (0,)), ((), ())), preferred_element_type=F32)

    def chunk(c, carry):
        rows = pl.ds(pl.multiple_of(c * CHUNK, CHUNK), CHUNK)
        r, lw, k, v = r_ref[rows, :], lw_ref[rows, :], k_ref[rows, :], v_ref[rows, :]
        kn, a = kn_ref[rows, :], a_ref[rows, :]
        cum = jnp.dot(tri, lw, precision=HIGHEST, preferred_element_type=F32)
        p_in = jnp.exp(cum)
        p_ex = jnp.exp(cum - lw)
        p_inv = jnp.exp(-cum)
        p_end = p_in[CHUNK - 1:CHUNK, :]
        at = -kn * p_ex
        bt = kn * a * p_inv
        kt = k * p_inv
        rt = r * p_in
        ar = jnp.concatenate([stack(at), stack(rt)], axis=0)
        bk = jnp.concatenate([stack(bt), stack(kt)], axis=0)
        vs = stack(v)
        gram = mm_nt(ar, bk)
        a_ab = jnp.where(strict, gram[0:w, 0:w], 0.0)
        a_ak = jnp.where(strict, gram[0:w, w:2 * w], 0.0)
        a_rb = jnp.where(incl, gram[w:2 * w, 0:w], 0.0)
        a_rk = jnp.where(incl, gram[w:2 * w, w:2 * w], 0.0)
        inv = eye + a_ab
        apow = a_ab.astype(BF16)
        for _ in range(int(math.log2(CHUNK)) - 1):
            apow_f = mm(apow, apow)
            apow = apow_f.astype(BF16)
            inv = inv + mm(inv.astype(BF16), apow)
        s0 = s_sc[...]
        xr = mm_nt(ar, s0.astype(BF16))
        av = mm(jnp.concatenate([a_ak, a_rk], axis=0).astype(BF16), vs)
        u = mm(inv.astype(BF16), (xr[0:w] + av[0:w]).astype(BF16))
        ub = u.astype(BF16)
        y4 = xr[w:2 * w] + mm(a_rb.astype(BF16), ub) + av[w:2 * w]
        y = y4[0:CHUNK]
        for h in range(1, SCAN_HEADS):
            y = y + y4[h * CHUNK:(h + 1) * CHUNK]
        y_ref[rows, :] = y
        s_sc[...] = s0 * p_end + mm_tn(ub, stack(bt * p_end)) + mm_tn(vs, stack(kt * p_end))
        return carry

    lax.fori_loop(0, tt // CHUNK, chunk, 0)


def _rwkv_scan(r, lw, k, v, kn, a, b, tp):
    n = r.shape[0]
    tt = _tile(tp, 640)
    nt = tp // tt
    spec = pl.BlockSpec((tt, SCAN_W), lambda bi, h, ti: (bi * nt + ti, h))
    kern = functools.partial(_scan_kernel, tt=tt)
    return pl.pallas_call(
        kern,
        grid=(b, WIDTH // SCAN_W, nt),
        in_specs=[spec] * 6,
        out_specs=spec,
        out_shape=jax.ShapeDtypeStruct((n, WIDTH), F32),
        scratch_shapes=[pltpu.VMEM((SCAN_W, SCAN_W), F32)],
        compiler_params=_params(("parallel", "parallel", "arbitrary")),
        name="rwkv_scan",
    )(r, lw, k, v, kn, a)


def _post_kernel(y_ref, g_ref, bg_ref, lg_ref, lb_ref, sel_ref, selt_ref, o_ref):
    y = y_ref[...]
    mean = _head_sum(y, sel_ref, selt_ref) * (1.0 / HEAD_DIM)
    yc = y - mean
    var = _head_sum(yc * yc, sel_ref, selt_ref) * (1.0 / HEAD_DIM)
    yn = yc * lax.rsqrt(var + GN_EPS) * lg_ref[...] + lb_ref[...]
    o_ref[...] = (yn * g_ref[...] + bg_ref[...]).astype(o_ref.dtype)


def _rwkv_post(y, g, bg, lnx_g, lnx_b, sel, selt, tp):
    n = y.shape[0]
    tm = _tile(tp, 320)
    row = pl.BlockSpec((tm, WIDTH), lambda i: (i, 0))
    full = lambda a: pl.BlockSpec(a.shape, lambda i: (0,) * a.ndim)
    return pl.pallas_call(
        _post_kernel,
        grid=(n // tm,),
        in_specs=[row, row, row, full(lnx_g), full(lnx_b), full(sel), full(selt)],
        out_specs=row,
        out_shape=jax.ShapeDtypeStruct((n, WIDTH), BF16),
        compiler_params=_params(("parallel",)),
        name="rwkv_post",
    )(y, g, bg, lnx_g, lnx_b, sel, selt)


def _outproj_kernel(fox_ref, rwkv_ref, h_ref, wt_ref, wb_ref, g_ref, b_ref, of_ref, ob_ref):
    mixed = (jnp.dot(fox_ref[...], wt_ref[...], preferred_element_type=F32)
             + jnp.dot(rwkv_ref[...], wb_ref[...], preferred_element_type=F32))
    y = _layer_norm(ALPHA * h_ref[...] + mixed, g_ref[...], b_ref[...])
    of_ref[...] = y
    ob_ref[...] = y.astype(BF16)


def _outproj_ln(fox, rwkv, h, w_top, w_bot, g, b, tp):
    n, d = h.shape
    tm = _tile(tp, 320)
    half = pl.BlockSpec((tm, WIDTH), lambda i: (i, 0))
    row = pl.BlockSpec((tm, d), lambda i: (i, 0))
    wsp = pl.BlockSpec((WIDTH, d), lambda i: (0, 0))
    vec = pl.BlockSpec((1, d), lambda i: (0, 0))
    return pl.pallas_call(
        _outproj_kernel,
        grid=(n // tm,),
        in_specs=[half, half, row, wsp, wsp, vec, vec],
        out_specs=[row, row],
        out_shape=[jax.ShapeDtypeStruct((n, d), F32), jax.ShapeDtypeStruct((n, d), BF16)],
        compiler_params=_params(("parallel",)),
        name="outproj_ln",
    )(fox, rwkv, h, w_top, w_bot, g, b)


def _ffn_kernel(xb_ref, xf_ref, w1_ref, w2_ref, g_ref, b_ref, of_ref, ob_ref, acc_sc):
    j = pl.program_id(1)

    @pl.when(j == 0)
    def _():
        acc_sc[...] = jnp.zeros_like(acc_sc)

    u = jnp.maximum(jnp.dot(xb_ref[...], w1_ref[...], preferred_element_type=F32), 0.0)
    acc_sc[...] += jnp.dot((u * u).astype(BF16), w2_ref[...], preferred_element_type=F32)

    @pl.when(j == pl.num_programs(1) - 1)
    def _():
        y = _layer_norm(ALPHA * xf_ref[...] + acc_sc[...], g_ref[...], b_ref[...])
        of_ref[...] = y
        ob_ref[...] = y.astype(BF16)


def _ffn_ln(xb, xf, w1, w2, g, b, tp):
    n, d = xf.shape
    tm = _tile(tp, 640)
    tf = 512
    row = pl.BlockSpec((tm, d), lambda i, j: (i, 0))
    vec = pl.BlockSpec((1, d), lambda i, j: (0, 0))
    return pl.pallas_call(
        _ffn_kernel,
        grid=(n // tm, D_FF // tf),
        in_specs=[row, row,
                  pl.BlockSpec((d, tf), lambda i, j: (0, j)),
                  pl.BlockSpec((tf, d), lambda i, j: (j, 0)),
                  vec, vec],
        out_specs=[row, row],
        out_shape=[jax.ShapeDtypeStruct((n, d), F32), jax.ShapeDtypeStruct((n, d), BF16)],
        scratch_shapes=[pltpu.VMEM((tm, d), F32)],
        compiler_params=_params(("parallel", "arbitrary")),
        name="ffn_ln",
    )(xb, xf, w1, w2, g, b)


def _pad_cols(a, width):
    return jnp.pad(a, ((0, 0), (0, width - a.shape[1])))


def _pad_rows(a, height):
    return jnp.pad(a, ((0, height - a.shape[0]), (0, 0)))


def _layer_params(l, w_in, mu, fox_fb, fox_out_g, w_up, w0, a_up, a0, g_up, v_up, v0,
                  k_k, k_a, r_k, lnx_g, lnx_b, w_out, ln1_g, ln1_b, w_ff1, w_ff2, ln2_g, ln2_b):
    first = v_up is None
    fq, fk, fv, ff = 0, WIDTH, 2 * WIDTH, 3 * WIDTH
    rs = 3 * WIDTH + HEADS
    o_w = 3 * WIDTH
    o_a = o_w + DECAY_RANK
    o_g = o_a + AAA_RANK
    o_v = o_g + GATE_RANK
    scale = HEAD_DIM ** -0.5
    w_fox = jnp.concatenate([w_in[:, fq:fk] * scale, w_in[:, fk:ff]], axis=1).astype(BF16)
    wr = w_in[:, rs:]
    w_big = wr[:, 0:o_w].astype(BF16)
    smalls = [(w_in[:, ff:rs], SM_W - SM_F), (wr[:, o_w:o_a], SM_A - SM_W), (wr[:, o_a:o_g], SM_G - SM_A),
              (wr[:, o_g:o_v], SM_V - SM_G)]
    mus = [(jnp.zeros((1, HEADS), F32), SM_W - SM_F), (mu[None, o_w:o_a], SM_A - SM_W),
           (mu[None, o_a:o_g], SM_G - SM_A), (mu[None, o_g:o_v], SM_V - SM_G)]
    if first:
        smalls.append((jnp.zeros((D_MODEL, 0), F32), SM_COLS - SM_V))
        mus.append((jnp.zeros((1, 0), F32), SM_COLS - SM_V))
    else:
        smalls.append((wr[:, o_v:], SM_COLS - SM_V))
        mus.append((mu[None, o_v:], SM_COLS - SM_V))
    w_small = jnp.concatenate([_pad_cols(a, wd) for a, wd in smalls], axis=1).astype(BF16)
    mu_small = jnp.concatenate([_pad_cols(a, wd) for a, wd in mus], axis=1)
    row = lambda a: a.reshape(1, -1)
    lp = dict(
        w_fox=w_fox, w_big=w_big, w_small=w_small,
        mu_big=mu[None, 0:o_w], mu_small=mu_small,
        fb=_pad_cols(fox_fb[None, :], LANES),
        gains=fox_out_g.reshape(HEADS // 2, 1, 2 * HEAD_DIM),
        w_up=_pad_rows(w_up, SM_A - SM_W).astype(BF16), w0=row(w0),
        a_up=_pad_rows(a_up, SM_G - SM_A).astype(BF16), a0=row(a0),
        g_up=_pad_rows(g_up, SM_V - SM_G).astype(BF16),
        k_k=row(k_k), k_a=row(k_a), r_k=row(r_k), lnx_g=row(lnx_g), lnx_b=row(lnx_b),
        w_top=w_out[:WIDTH].astype(BF16), w_bot=w_out[WIDTH:].astype(BF16),
        ln1_g=row(ln1_g), ln1_b=row(ln1_b),
        w_ff1=w_ff1.astype(BF16), w_ff2=w_ff2.astype(BF16),
        ln2_g=row(ln2_g), ln2_b=row(ln2_b),
    )
    if not first:
        lp["v_up"] = _pad_rows(v_up, SM_COLS - SM_V).astype(BF16)
        lp["v0"] = row(v0)
    return lp


def _layer(hf, hb, lp, sel, selt, v_first, b, tp):
    qkv = _matmul(hb, lp["w_fox"], BF16, tp, "inproj_fox")
    big = _matmul(hb, lp["w_big"], F32, tp, "inproj_rwkv")
    small = _matmul(hb, lp["w_small"], F32, tp, "inproj_small")
    c = _fox_cumsum(small, lp["fb"], b, tp)
    fox = _fox_attention(qkv, c.reshape(b, HEADS // 2, 2, tp), lp["gains"], b, tp)
    outs = _rwkv_prep(big, small, lp, sel, selt, v_first, tp)
    r, lw, k, v, kn, a, g, bg = outs[:8]
    if v_first is None:
        v_first = outs[8]
    y = _rwkv_scan(r, lw, k, v, kn, a, b, tp)
    rwkv = _rwkv_post(y, g, bg, lp["lnx_g"], lp["lnx_b"], sel, selt, tp)
    xf, xb = _outproj_ln(fox, rwkv, hf, lp["w_top"], lp["w_bot"], lp["ln1_g"], lp["ln1_b"], tp)
    hf, hb = _ffn_ln(xb, xf, lp["w_ff1"], lp["w_ff2"], lp["ln2_g"], lp["ln2_b"], tp)
    return hf, hb, v_first


def kernel(x, meta, ln_in_g, ln_in_b, w_in_first, w_in_rest, mu_first, mu_rest, fox_fb, fox_out_g, w_up, w0, a_up, a0, g_up, v_up, v0, k_k, k_a, r_k, lnx_g, lnx_b, w_out, ln1_g, ln1_b, w_ff1, w_ff2, ln2_g, ln2_b):
    b, seq, d = x.shape
    assert d == D_MODEL
    tp = FRONT_PAD + N_META + seq
    assert tp % LANES == 0
    head = jnp.concatenate([jnp.zeros((FRONT_PAD, d), x.dtype), meta.astype(x.dtype)], axis=0)
    h0 = jnp.concatenate([jnp.broadcast_to(head[None], (b, LANES, d)), x], axis=1).reshape(b * tp, d)
    hf, hb = _ln_rows(h0, ln_in_g, ln_in_b, tp)

    lane_head = jnp.arange(WIDTH, dtype=jnp.int32) // HEAD_DIM
    sel = (lane_head[:, None] == jnp.arange(LANES, dtype=jnp.int32)[None, :]).astype(F32)
    selt = sel.T

    v_first = None
    for l in range(DEPTH):
        first = l == 0
        lp = _layer_params(
            l, w_in_first if first else w_in_rest[l - 1], mu_first if first else mu_rest[l - 1],
            fox_fb[l], fox_out_g[l], w_up[l], w0[l], a_up[l], a0[l], g_up[l],
            None if first else v_up[l - 1], None if first else v0[l - 1],
            k_k[l], k_a[l], r_k[l], lnx_g[l], lnx_b[l], w_out[l],
            ln1_g[l], ln1_b[l], w_ff1[l], w_ff2[l], ln2_g[l], ln2_b[l])
        hf, hb, v_first = _layer(hf, hb, lp, sel, selt, v_first, b, tp)
    return hf.reshape(b, tp, d)[:, LANES:, :]
```

```python
import functools
import math

import jax
import jax.numpy as jnp
from jax import lax
from jax.experimental import pallas as pl
from jax.experimental.pallas import tpu as pltpu

F32 = jnp.float32
BF16 = jnp.bfloat16
HIGHEST = lax.Precision.HIGHEST

D_MODEL = 2048
DEPTH = 4
N_META = 16
LANES = 128
FRONT_PAD = (-N_META) % LANES
HEAD_DIM = 64
WIDTH = D_MODEL // 2
HEADS = WIDTH // HEAD_DIM
DECAY_RANK = 64
AAA_RANK = 64
GATE_RANK = 160
MV_RANK = 32
D_FF = 4 * D_MODEL
ALPHA = (2 * DEPTH) ** 0.25
LN_EPS = 1e-5
GN_EPS = 64e-5
OUT_NORM_EPS = 1e-6
DECAY_SCALE = math.exp(-0.5)
NEG_INF = -1e30
MASKED_KEY = 1e30
LOG2E = math.log2(math.e)
C_PARTS = 3

SM_F, SM_W, SM_A, SM_G, SM_V, SM_COLS = 0, 128, 256, 384, 640, 768

CHUNK = 64
SCAN_HEADS = 4
SCAN_W = SCAN_HEADS * HEAD_DIM
VMEM_LIMIT = 56 * 1024 * 1024


def _tile(total, cap, mult=64):
    best = None
    for d in range(mult, cap + 1, mult):
        if total % d == 0:
            best = d
    assert best is not None, (total, cap)
    return best


def _params(sem):
    return pltpu.CompilerParams(dimension_semantics=sem, vmem_limit_bytes=VMEM_LIMIT)


def _layer_norm(x, g, b):
    mean = jnp.mean(x, -1, keepdims=True)
    xc = x - mean
    var = jnp.mean(xc * xc, -1, keepdims=True)
    return xc * lax.rsqrt(var + LN_EPS) * g + b


def _ln_kernel(x_ref, g_ref, b_ref, of_ref, ob_ref):
    y = _layer_norm(x_ref[...], g_ref[...], b_ref[...])
    of_ref[...] = y
    ob_ref[...] = y.astype(BF16)


def _ln_rows(x, g, b, tp):
    n, d = x.shape
    tm = _tile(tp, 320)
    row = pl.BlockSpec((tm, d), lambda i: (i, 0))
    vec = pl.BlockSpec((1, d), lambda i: (0, 0))
    return pl.pallas_call(
        _ln_kernel,
        grid=(n // tm,),
        in_specs=[row, vec, vec],
        out_specs=[row, row],
        out_shape=[jax.ShapeDtypeStruct((n, d), F32), jax.ShapeDtypeStruct((n, d), BF16)],
        compiler_params=_params(("parallel",)),
        name="ln_in",
    )(x, g.reshape(1, d), b.reshape(1, d))


def _mm_kernel(x_ref, w_ref, o_ref):
    o_ref[...] = jnp.dot(x_ref[...], w_ref[...], preferred_element_type=F32).astype(o_ref.dtype)


def _matmul(x, w, out_dtype, tp, name):
    n, k = x.shape
    nc = w.shape[1]
    tm = _tile(tp, 640)
    tn = _tile(nc, 768, LANES)
    return pl.pallas_call(
        _mm_kernel,
        grid=(n // tm, nc // tn),
        in_specs=[pl.BlockSpec((tm, k), lambda i, j: (i, 0)),
                  pl.BlockSpec((k, tn), lambda i, j: (0, j))],
        out_specs=pl.BlockSpec((tm, tn), lambda i, j: (i, j)),
        out_shape=jax.ShapeDtypeStruct((n, nc), out_dtype),
        compiler_params=_params(("parallel", "parallel")),
        name=name,
    )(x, w)


def _cumsum_kernel(f_ref, fb_ref, o_ref, carry_ref):
    t = pl.program_id(1)

    @pl.when(t == 0)
    def _():
        carry_ref[...] = jnp.zeros_like(carry_ref)

    z = f_ref[...] + fb_ref[...]
    lf = jnp.minimum(z, 0.0) - jnp.log1p(jnp.exp(-jnp.abs(z)))
    ri = lax.broadcasted_iota(jnp.int32, (LANES, LANES), 0)
    ci = lax.broadcasted_iota(jnp.int32, (LANES, LANES), 1)
    real = (t * LANES + ri) >= FRONT_PAD
    lf = jnp.where(real, lf, 0.0)
    tri = (ci <= ri).astype(F32)
    c = jnp.dot(tri, lf, precision=HIGHEST, preferred_element_type=F32) + carry_ref[...]
    carry_ref[...] = c[LANES - 1:LANES, :]
    o_ref[...] = jnp.where(real, c, MASKED_KEY)


def _fox_cumsum(small, fb_pad, b, tp):
    nblk = tp // LANES
    blk = pl.BlockSpec((LANES, LANES), lambda bi, t: (bi * nblk + t, 0))
    return pl.pallas_call(
        _cumsum_kernel,
        grid=(b, nblk),
        in_specs=[blk, pl.BlockSpec((1, LANES), lambda bi, t: (0, 0))],
        out_specs=blk,
        out_shape=jax.ShapeDtypeStruct((b * tp, LANES), F32),
        scratch_shapes=[pltpu.VMEM((1, LANES), F32)],
        compiler_params=_params(("parallel", "arbitrary")),
        name="fox_cumsum",
    )(small, fb_pad)


def _fox_prep_kernel(qkv_ref, c_ref, qo_ref, ko_ref, vo_ref, *, tm):
    lane = lax.broadcasted_iota(jnp.int32, (tm, LANES), 1)
    low = lane < HEAD_DIM
    cs = c_ref[...] * LOG2E
    hi = cs.astype(BF16).astype(F32)
    rest = cs - hi
    mid = rest.astype(BF16).astype(F32)
    lo = rest - mid
    for j in range(HEADS // 2):
        qb = qkv_ref[:, j * LANES:(j + 1) * LANES].astype(F32)
        kb = qkv_ref[:, WIDTH + j * LANES:WIDTH + (j + 1) * LANES].astype(F32)
        vb = qkv_ref[:, 2 * WIDTH + j * LANES:2 * WIDTH + (j + 1) * LANES].astype(F32)
        for hh in range(2):
            h = 2 * j + hh
            data = low if hh == 0 else jnp.logical_not(low)
            x0 = HEAD_DIM if hh == 0 else 0
            ones = jnp.where((lane >= x0) & (lane < x0 + C_PARTS), 1.0, 0.0)
            cpart = jnp.where(lane == x0, -hi[:, h:h + 1],
                              jnp.where(lane == x0 + 1, -mid[:, h:h + 1],
                                        jnp.where(lane == x0 + 2, -lo[:, h:h + 1], 0.0)))
            qo_ref[0, h] = jnp.where(data, qb, ones).astype(BF16)
            ko_ref[0, h] = jnp.where(data, kb, cpart).astype(BF16)
            vo_ref[0, h] = jnp.where(data, vb, jnp.where(lane == x0, 1.0, 0.0)).astype(BF16)


def _fox_prep(qkv, c, b, tp):
    tm = _tile(tp, 320)
    nt = tp // tm
    out = pl.BlockSpec((1, HEADS, tm, LANES), lambda bi, t: (bi, 0, t, 0))
    shape = jax.ShapeDtypeStruct((b, HEADS, tp, LANES), BF16)
    return pl.pallas_call(
        functools.partial(_fox_prep_kernel, tm=tm),
        grid=(b, nt),
        in_specs=[pl.BlockSpec((tm, 3 * WIDTH), lambda bi, t: (bi * nt + t, 0)),
                  pl.BlockSpec((tm, LANES), lambda bi, t: (bi * nt + t, 0))],
        out_specs=[out, out, out],
        out_shape=[shape, shape, shape],
        compiler_params=_params(("parallel", "parallel")),
        name="fox_prep",
    )(qkv, c)


def _fox_kernel(q_ref, k_ref, v_ref, g_ref, o_ref, m_sc, acc_sc, *, blk):
    qi = pl.program_id(2)
    reps = blk // LANES
    ri = lax.broadcasted_iota(jnp.int32, (blk, blk), 0)
    ci = lax.broadcasted_iota(jnp.int32, (blk, blk), 1)
    causal = ci <= ri
    for h in range(2):
        m_sc[h] = jnp.full((blk, LANES), NEG_INF, F32)
        acc_sc[h] = jnp.zeros((blk, LANES), F32)

    def step(j, masked):
        rows = pl.ds(pl.multiple_of(j * blk, LANES), blk)
        scores = [lax.dot_general(q_ref[0, h], k_ref[0, h, rows, :], (((1,), (1,)), ((), ())),
                                  preferred_element_type=F32) for h in range(2)]
        for h in range(2):
            s = scores[h]
            if masked:
                s = jnp.where(causal, s, NEG_INF)
            m_prev = m_sc[h]
            m_next = jnp.maximum(m_prev, jnp.max(s, axis=1)[:, None])
            p = jnp.exp2(s - jnp.tile(m_next, (1, reps)))
            acc_sc[h] = (jnp.exp2(m_prev - m_next) * acc_sc[h]
                         + jnp.dot(p.astype(BF16), v_ref[0, h, rows, :], preferred_element_type=F32))
            m_sc[h] = m_next

    def body(j, carry):
        step(j, False)
        return carry

    lax.fori_loop(0, qi, body, 0)
    step(qi, True)

    lane = lax.broadcasted_iota(jnp.int32, (blk, LANES), 1)
    low = lane < HEAD_DIM
    outs = []
    for h in range(2):
        data = low if h == 0 else jnp.logical_not(low)
        x0 = HEAD_DIM if h == 0 else 0
        acc = acc_sc[h]
        o = jnp.where(data, acc / acc[:, x0:x0 + 1], 0.0)
        ms = jnp.sum(o * o, axis=1, keepdims=True) * (1.0 / HEAD_DIM)
        outs.append(o * lax.rsqrt(ms + OUT_NORM_EPS))
    o_ref[...] = ((outs[0] + outs[1]) * g_ref[0]).astype(o_ref.dtype)


def _fox_attention(qp, kp, vp, gains, b, tp):
    blk = _tile(tp, 640, LANES)
    nq = tp // blk
    hp = HEADS // 2
    kern = functools.partial(_fox_kernel, blk=blk)
    kv = pl.BlockSpec((1, 2, tp, LANES), lambda bi, h, qi: (bi, h, 0, 0))
    return pl.pallas_call(
        kern,
        grid=(b, hp, nq),
        in_specs=[pl.BlockSpec((1, 2, blk, LANES), lambda bi, h, qi: (bi, h, qi, 0)),
                  kv, kv,
                  pl.BlockSpec((1, 1, LANES), lambda bi, h, qi: (h, 0, 0))],
        out_specs=pl.BlockSpec((blk, LANES), lambda bi, h, qi: (bi * nq + qi, h)),
        out_shape=jax.ShapeDtypeStruct((b * tp, WIDTH), BF16),
        scratch_shapes=[pltpu.VMEM((2, blk, LANES), F32), pltpu.VMEM((2, blk, LANES), F32)],
        compiler_params=_params(("parallel", "parallel", "arbitrary")),
        name="fox_attention",
    )(qp, kp, vp, gains)


def _head_sum(x, sel_ref, selt_ref):
    s = jnp.dot(x, sel_ref[...], precision=HIGHEST, preferred_element_type=F32)
    return jnp.dot(s, selt_ref[...], precision=HIGHEST, preferred_element_type=F32)


def _prep_kernel(*refs, tm, tiles_per_batch, first):
    if first:
        (big_ref, sm_ref, mub_ref, mus_ref, wup_ref, w0_ref, aup_ref, a0_ref, gup_ref,
         kk_ref, ka_ref, rk_ref, sel_ref, selt_ref,
         r_o, lw_o, k_o, v_o, kn_o, a_o, g_o, bg_o, vf_o, cb_sc, cs_sc) = refs
    else:
        (big_ref, sm_ref, mub_ref, mus_ref, wup_ref, w0_ref, aup_ref, a0_ref, gup_ref,
         vup_ref, v0_ref, vf_ref, kk_ref, ka_ref, rk_ref, sel_ref, selt_ref,
         r_o, lw_o, k_o, v_o, kn_o, a_o, g_o, bg_o, cb_sc, cs_sc) = refs
    i = pl.program_id(0)

    @pl.when(i == 0)
    def _():
        cb_sc[...] = jnp.zeros_like(cb_sc)
        cs_sc[...] = jnp.zeros_like(cs_sc)

    ri = lax.broadcasted_iota(jnp.int32, (tm, 1), 0)
    real = (lax.rem(i, tiles_per_batch) * tm + ri) >= FRONT_PAD

    def mix(x_ref, mu_ref, carry):
        h = jnp.where(real, x_ref[...], 0.0)
        prev = jnp.where(ri == 0, carry[...], pltpu.roll(h, 1, axis=0))
        carry[...] = h[tm - 1:tm, :]
        p = h + mu_ref[...] * (prev - h)
        return jnp.where(real, p, 0.0)

    pb = mix(big_ref, mub_ref, cb_sc)
    ps = mix(sm_ref, mus_ref, cs_sc)
    r = pb[:, 0:WIDTH]
    k = pb[:, WIDTH:2 * WIDTH]
    v = pb[:, 2 * WIDTH:3 * WIDTH]

    def lora(x, w_ref):
        return jnp.dot(x.astype(BF16), w_ref[...], preferred_element_type=F32)

    lw = -DECAY_SCALE * jax.nn.sigmoid(w0_ref[...] + lora(jnp.tanh(ps[:, SM_W:SM_A]), wup_ref))
    a = jax.nn.sigmoid(a0_ref[...] + lora(ps[:, SM_A:SM_G], aup_ref))
    g = lora(jax.nn.sigmoid(ps[:, SM_G:SM_V]), gup_ref)
    if first:
        vf_o[...] = v
    else:
        v = v + (vf_ref[...] - v) * jax.nn.sigmoid(v0_ref[...] + lora(ps[:, SM_V:SM_COLS], vup_ref))

    kn = k * kk_ref[...]
    ss = jnp.dot(kn * kn, sel_ref[...], precision=HIGHEST, preferred_element_type=F32)
    inv = 1.0 / jnp.maximum(jnp.sqrt(ss), 1e-12)
    kn = kn * jnp.dot(inv, selt_ref[...], precision=HIGHEST, preferred_element_type=F32)
    k = k * (1.0 + (a - 1.0) * ka_ref[...])
    bonus = _head_sum(r * k * rk_ref[...], sel_ref, selt_ref) * v

    r_o[...] = r
    lw_o[...] = lw
    k_o[...] = k
    v_o[...] = v
    kn_o[...] = kn
    a_o[...] = a
    g_o[...] = g
    bg_o[...] = bonus * g


def _rwkv_prep(big, small, lp, sel, selt, v_first, tp):
    n = big.shape[0]
    first = v_first is None
    tm = _tile(tp, 128)
    row = lambda c: pl.BlockSpec((tm, c), lambda i: (i, 0))
    full = lambda a: pl.BlockSpec(a.shape, lambda i: (0,) * a.ndim)
    ins = [big, small, lp["mu_big"], lp["mu_small"], lp["w_up"], lp["w0"], lp["a_up"], lp["a0"], lp["g_up"]]
    specs = [row(3 * WIDTH), row(SM_COLS)] + [full(a) for a in ins[2:]]
    if not first:
        ins += [lp["v_up"], lp["v0"], v_first]
        specs += [full(lp["v_up"]), full(lp["v0"]), row(WIDTH)]
    tail = [lp["k_k"], lp["k_a"], lp["r_k"], sel, selt]
    ins += tail
    specs += [full(a) for a in tail]
    n_out = 9 if first else 8
    kern = functools.partial(_prep_kernel, tm=tm, tiles_per_batch=tp // tm, first=first)
    return pl.pallas_call(
        kern,
        grid=(n // tm,),
        in_specs=specs,
        out_specs=[row(WIDTH)] * n_out,
        out_shape=[jax.ShapeDtypeStruct((n, WIDTH), F32)] * n_out,
        scratch_shapes=[pltpu.VMEM((1, 3 * WIDTH), F32), pltpu.VMEM((1, SM_COLS), F32)],
        compiler_params=_params(("arbitrary",)),
        name="rwkv_prep",
    )(*ins)


def _scan_kernel(r_ref, lw_ref, k_ref, v_ref, kn_ref, a_ref, y_ref, s_sc, *, tt):
    ti = pl.program_id(1)

    @pl.when(ti == 0)
    def _():
        s_sc[...] = jnp.zeros_like(s_sc)

    w = SCAN_W
    ri = lax.broadcasted_iota(jnp.int32, (w, w), 0)
    ci = lax.broadcasted_iota(jnp.int32, (w, w), 1)
    same_head = (ri // HEAD_DIM) == (ci // HEAD_DIM)
    strict = same_head & (ci < ri)
    incl = same_head & (ci <= ri)
    eye = (ri == ci).astype(F32)
    tri_r = lax.broadcasted_iota(jnp.int32, (CHUNK, CHUNK), 0)
    tri_c = lax.broadcasted_iota(jnp.int32, (CHUNK, CHUNK), 1)
    tri = (tri_c <= tri_r).astype(F32)

    def stack(x):
        return jnp.where(same_head, jnp.tile(x, (SCAN_HEADS, 1)), 0.0).astype(BF16)

    def mm(x, y):
        return jnp.dot(x, y, preferred_element_type=F32)

    def mm_nt(x, y):
        return lax.dot_general(x, y, (((1,), (1,)), ((), ())), preferred_element_type=F32)

    def mm_tn(x, y):
        return lax.dot_general(x, y, (((0,), (0,)), ((), ())), preferred_element_type=F32)

    groups = range(WIDTH // w)

    def prepare(rows, gi):
        cols = slice(gi * w, (gi + 1) * w)
        r, lw, k, v = r_ref[rows, cols], lw_ref[rows, cols], k_ref[rows, cols], v_ref[rows, cols]
        kn, a = kn_ref[rows, cols], a_ref[rows, cols]
        cum = jnp.dot(tri, lw, precision=HIGHEST, preferred_element_type=F32)
        p_in = jnp.exp(cum)
        p_ex = jnp.exp(cum - lw)
        p_inv = jnp.exp(-cum)
        p_end = p_in[CHUNK - 1:CHUNK, :]
        bt = kn * a * p_inv
        kt = k * p_inv
        ar = jnp.concatenate([stack(-kn * p_ex), stack(r * p_in)], axis=0)
        bk = jnp.concatenate([stack(bt), stack(kt)], axis=0)
        return dict(ar=ar, bk=bk, vs=stack(v), p_end=p_end,
                    bh=stack(bt * p_end), kh=stack(kt * p_end))

    def chunk(c, carry):
        rows = pl.ds(pl.multiple_of(c * CHUNK, CHUNK), CHUNK)
        g = [prepare(rows, gi) for gi in groups]
        gram = [mm_nt(x["ar"], x["bk"]) for x in g]
        a_ab = [jnp.where(strict, m[0:w, 0:w], 0.0) for m in gram]
        a_kv = [jnp.concatenate([jnp.where(strict, m[0:w, w:2 * w], 0.0),
                                 jnp.where(incl, m[w:2 * w, w:2 * w], 0.0)], axis=0).astype(BF16)
                for m in gram]
        a_rb = [jnp.where(incl, m[w:2 * w, 0:w], 0.0).astype(BF16) for m in gram]
        s0 = [s_sc[gi] for gi in groups]
        xr = [mm_nt(x["ar"], s.astype(BF16)) for x, s in zip(g, s0)]
        av = [mm(m, x["vs"]) for m, x in zip(a_kv, g)]
        inv = [eye + m for m in a_ab]
        apow = [m.astype(BF16) for m in a_ab]
        for _ in range(int(math.log2(CHUNK)) - 1):
            apow = [mm(m, m).astype(BF16) for m in apow]
            inv = [t + mm(t.astype(BF16), m) for t, m in zip(inv, apow)]
        ub = [mm(t.astype(BF16), (x_[0:w] + v_[0:w]).astype(BF16)).astype(BF16)
              for t, x_, v_ in zip(inv, xr, av)]
        y4 = [x_[w:2 * w] + mm(m, u_) + v_[w:2 * w] for x_, m, u_, v_ in zip(xr, a_rb, ub, av)]
        for gi in groups:
            y = y4[gi][0:CHUNK]
            for h in range(1, SCAN_HEADS):
                y = y + y4[gi][h * CHUNK:(h + 1) * CHUNK]
            y_ref[rows, gi * w:(gi + 1) * w] = y
        for gi in groups:
            s_sc[gi] = (s0[gi] * g[gi]["p_end"] + mm_tn(ub[gi], g[gi]["bh"])
                        + mm_tn(g[gi]["vs"], g[gi]["kh"]))
        return carry

    lax.fori_loop(0, tt // CHUNK, chunk, 0)


def _rwkv_scan(r, lw, k, v, kn, a, b, tp):
    n = r.shape[0]
    tt = _tile(tp, 320)
    nt = tp // tt
    spec = pl.BlockSpec((tt, WIDTH), lambda bi, ti: (bi * nt + ti, 0))
    kern = functools.partial(_scan_kernel, tt=tt)
    return pl.pallas_call(
        kern,
        grid=(b, nt),
        in_specs=[spec] * 6,
        out_specs=spec,
        out_shape=jax.ShapeDtypeStruct((n, WIDTH), F32),
        scratch_shapes=[pltpu.VMEM((WIDTH // SCAN_W, SCAN_W, SCAN_W), F32)],
        compiler_params=_params(("parallel", "arbitrary")),
        name="rwkv_scan",
    )(r, lw, k, v, kn, a)


def _post_kernel(y_ref, g_ref, bg_ref, lg_ref, lb_ref, sel_ref, selt_ref, o_ref):
    y = y_ref[...]
    mean = _head_sum(y, sel_ref, selt_ref) * (1.0 / HEAD_DIM)
    yc = y - mean
    var = _head_sum(yc * yc, sel_ref, selt_ref) * (1.0 / HEAD_DIM)
    yn = yc * lax.rsqrt(var + GN_EPS) * lg_ref[...] + lb_ref[...]
    o_ref[...] = (yn * g_ref[...] + bg_ref[...]).astype(o_ref.dtype)


def _rwkv_post(y, g, bg, lnx_g, lnx_b, sel, selt, tp):
    n = y.shape[0]
    tm = _tile(tp, 320)
    row = pl.BlockSpec((tm, WIDTH), lambda i: (i, 0))
    full = lambda a: pl.BlockSpec(a.shape, lambda i: (0,) * a.ndim)
    return pl.pallas_call(
        _post_kernel,
        grid=(n // tm,),
        in_specs=[row, row, row, full(lnx_g), full(lnx_b), full(sel), full(selt)],
        out_specs=row,
        out_shape=jax.ShapeDtypeStruct((n, WIDTH), BF16),
        compiler_params=_params(("parallel",)),
        name="rwkv_post",
    )(y, g, bg, lnx_g, lnx_b, sel, selt)


def _outproj_kernel(fox_ref, rwkv_ref, h_ref, wt_ref, wb_ref, g_ref, b_ref, of_ref, ob_ref):
    mixed = (jnp.dot(fox_ref[...], wt_ref[...], preferred_element_type=F32)
             + jnp.dot(rwkv_ref[...], wb_ref[...], preferred_element_type=F32))
    y = _layer_norm(ALPHA * h_ref[...] + mixed, g_ref[...], b_ref[...])
    of_ref[...] = y
    ob_ref[...] = y.astype(BF16)


def _outproj_ln(fox, rwkv, h, w_top, w_bot, g, b, tp):
    n, d = h.shape
    tm = _tile(tp, 320)
    half = pl.BlockSpec((tm, WIDTH), lambda i: (i, 0))
    row = pl.BlockSpec((tm, d), lambda i: (i, 0))
    wsp = pl.BlockSpec((WIDTH, d), lambda i: (0, 0))
    vec = pl.BlockSpec((1, d), lambda i: (0, 0))
    return pl.pallas_call(
        _outproj_kernel,
        grid=(n // tm,),
        in_specs=[half, half, row, wsp, wsp, vec, vec],
        out_specs=[row, row],
        out_shape=[jax.ShapeDtypeStruct((n, d), F32), jax.ShapeDtypeStruct((n, d), BF16)],
        compiler_params=_params(("parallel",)),
        name="outproj_ln",
    )(fox, rwkv, h, w_top, w_bot, g, b)


def _ffn_kernel(xb_ref, xf_ref, w1_ref, w2_ref, g_ref, b_ref, of_ref, ob_ref, acc_sc):
    j = pl.program_id(1)

    @pl.when(j == 0)
    def _():
        acc_sc[...] = jnp.zeros_like(acc_sc)

    u = jnp.maximum(jnp.dot(xb_ref[...], w1_ref[...], preferred_element_type=F32), 0.0)
    acc_sc[...] += jnp.dot((u * u).astype(BF16), w2_ref[...], preferred_element_type=F32)

    @pl.when(j == pl.num_programs(1) - 1)
    def _():
        y = _layer_norm(ALPHA * xf_ref[...] + acc_sc[...], g_ref[...], b_ref[...])
        of_ref[...] = y
        ob_ref[...] = y.astype(BF16)


def _ffn_ln(xb, xf, w1, w2, g, b, tp):
    n, d = xf.shape
    tm = _tile(tp, 640)
    tf = 512
    row = pl.BlockSpec((tm, d), lambda i, j: (i, 0))
    vec = pl.BlockSpec((1, d), lambda i, j: (0, 0))
    return pl.pallas_call(
        _ffn_kernel,
        grid=(n // tm, D_FF // tf),
        in_specs=[row, row,
                  pl.BlockSpec((d, tf), lambda i, j: (0, j)),
                  pl.BlockSpec((tf, d), lambda i, j: (j, 0)),
                  vec, vec],
        out_specs=[row, row],
        out_shape=[jax.ShapeDtypeStruct((n, d), F32), jax.ShapeDtypeStruct((n, d), BF16)],
        scratch_shapes=[pltpu.VMEM((tm, d), F32)],
        compiler_params=_params(("parallel", "arbitrary")),
        name="ffn_ln",
    )(xb, xf, w1, w2, g, b)


def _pad_cols(a, width):
    return jnp.pad(a, ((0, 0), (0, width - a.shape[1])))


def _pad_rows(a, height):
    return jnp.pad(a, ((0, height - a.shape[0]), (0, 0)))


def _layer_params(l, w_in, mu, fox_fb, fox_out_g, w_up, w0, a_up, a0, g_up, v_up, v0,
                  k_k, k_a, r_k, lnx_g, lnx_b, w_out, ln1_g, ln1_b, w_ff1, w_ff2, ln2_g, ln2_b):
    first = v_up is None
    fq, fk, fv, ff = 0, WIDTH, 2 * WIDTH, 3 * WIDTH
    rs = 3 * WIDTH + HEADS
    o_w = 3 * WIDTH
    o_a = o_w + DECAY_RANK
    o_g = o_a + AAA_RANK
    o_v = o_g + GATE_RANK
    scale = HEAD_DIM ** -0.5 * LOG2E
    w_fox = jnp.concatenate([w_in[:, fq:fk] * scale, w_in[:, fk:ff]], axis=1).astype(BF16)
    wr = w_in[:, rs:]
    w_big = wr[:, 0:o_w].astype(BF16)
    smalls = [(w_in[:, ff:rs], SM_W - SM_F), (wr[:, o_w:o_a], SM_A - SM_W), (wr[:, o_a:o_g], SM_G - SM_A),
              (wr[:, o_g:o_v], SM_V - SM_G)]
    mus = [(jnp.zeros((1, HEADS), F32), SM_W - SM_F), (mu[None, o_w:o_a], SM_A - SM_W),
           (mu[None, o_a:o_g], SM_G - SM_A), (mu[None, o_g:o_v], SM_V - SM_G)]
    if first:
        smalls.append((jnp.zeros((D_MODEL, 0), F32), SM_COLS - SM_V))
        mus.append((jnp.zeros((1, 0), F32), SM_COLS - SM_V))
    else:
        smalls.append((wr[:, o_v:], SM_COLS - SM_V))
        mus.append((mu[None, o_v:], SM_COLS - SM_V))
    w_small = jnp.concatenate([_pad_cols(a, wd) for a, wd in smalls], axis=1).astype(BF16)
    mu_small = jnp.concatenate([_pad_cols(a, wd) for a, wd in mus], axis=1)
    row = lambda a: a.reshape(1, -1)
    lp = dict(
        w_fox=w_fox, w_big=w_big, w_small=w_small,
        mu_big=mu[None, 0:o_w], mu_small=mu_small,
        fb=_pad_cols(fox_fb[None, :], LANES),
        gains=fox_out_g.reshape(HEADS // 2, 1, 2 * HEAD_DIM),
        w_up=_pad_rows(w_up, SM_A - SM_W).astype(BF16), w0=row(w0),
        a_up=_pad_rows(a_up, SM_G - SM_A).astype(BF16), a0=row(a0),
        g_up=_pad_rows(g_up, SM_V - SM_G).astype(BF16),
        k_k=row(k_k), k_a=row(k_a), r_k=row(r_k), lnx_g=row(lnx_g), lnx_b=row(lnx_b),
        w_top=w_out[:WIDTH].astype(BF16), w_bot=w_out[WIDTH:].astype(BF16),
        ln1_g=row(ln1_g), ln1_b=row(ln1_b),
        w_ff1=w_ff1.astype(BF16), w_ff2=w_ff2.astype(BF16),
        ln2_g=row(ln2_g), ln2_b=row(ln2_b),
    )
    if not first:
        lp["v_up"] = _pad_rows(v_up, SM_COLS - SM_V).astype(BF16)
        lp["v0"] = row(v0)
    return lp


def _layer(hf, hb, lp, sel, selt, v_first, b, tp):
    qkv = _matmul(hb, lp["w_fox"], BF16, tp, "inproj_fox")
    big = _matmul(hb, lp["w_big"], F32, tp, "inproj_rwkv")
    small = _matmul(hb, lp["w_small"], F32, tp, "inproj_small")
    c = _fox_cumsum(small, lp["fb"], b, tp)
    qp, kp, vp = _fox_prep(qkv, c, b, tp)
    fox = _fox_attention(qp, kp, vp, lp["gains"], b, tp)
    outs = _rwkv_prep(big, small, lp, sel, selt, v_first, tp)
    r, lw, k, v, kn, a, g, bg = outs[:8]
    if v_first is None:
        v_first = outs[8]
    y = _rwkv_scan(r, lw, k, v, kn, a, b, tp)
    rwkv = _rwkv_post(y, g, bg, lp["lnx_g"], lp["lnx_b"], sel, selt, tp)
    xf, xb = _outproj_ln(fox, rwkv, hf, lp["w_top"], lp["w_bot"], lp["ln1_g"], lp["ln1_b"], tp)
    hf, hb = _ffn_ln(xb, xf, lp["w_ff1"], lp["w_ff2"], lp["ln2_g"], lp["ln2_b"], tp)
    return hf, hb, v_first


def kernel(x, meta, ln_in_g, ln_in_b, w_in_first, w_in_rest, mu_first, mu_rest, fox_fb, fox_out_g, w_up, w0, a_up, a0, g_up, v_up, v0, k_k, k_a, r_k, lnx_g, lnx_b, w_out, ln1_g, ln1_b, w_ff1, w_ff2, ln2_g, ln2_b):
    b, seq, d = x.shape
    assert d == D_MODEL
    tp = FRONT_PAD + N_META + seq
    assert tp % LANES == 0
    head = jnp.concatenate([jnp.zeros((FRONT_PAD, d), x.dtype), meta.astype(x.dtype)], axis=0)
    h0 = jnp.concatenate([jnp.broadcast_to(head[None], (b, LANES, d)), x], axis=1).reshape(b * tp, d)
    hf, hb = _ln_rows(h0, ln_in_g, ln_in_b, tp)

    lane_head = jnp.arange(WIDTH, dtype=jnp.int32) // HEAD_DIM
    sel = (lane_head[:, None] == jnp.arange(LANES, dtype=jnp.int32)[None, :]).astype(F32)
    selt = sel.T

    v_first = None
    for l in range(DEPTH):
        first = l == 0
        lp = _layer_params(
            l, w_in_first if first else w_in_rest[l - 1], mu_first if first else mu_rest[l - 1],
            fox_fb[l], fox_out_g[l], w_up[l], w0[l], a_up[l], a0[l], g_up[l],
            None if first else v_up[l - 1], None if first else v0[l - 1],
            k_k[l], k_a[l], r_k[l], lnx_g[l], lnx_b[l], w_out[l],
            ln1_g[l], ln1_b[l], w_ff1[l], w_ff2[l], ln2_g[l], ln2_b[l])
        hf, hb, v_first = _layer(hf, hb, lp, sel, selt, v_first, b, tp)
    return hf.reshape(b, tp, d)[:, LANES:, :]
```

```python
import functools
import math

import jax
import jax.numpy as jnp
from jax import lax
from jax.experimental import pallas as pl
from jax.experimental.pallas import tpu as pltpu

F32 = jnp.float32
BF16 = jnp.bfloat16
HIGHEST = lax.Precision.HIGHEST

D_MODEL = 2048
DEPTH = 4
N_META = 16
LANES = 128
FRONT_PAD = (-N_META) % LANES
HEAD_DIM = 64
WIDTH = D_MODEL // 2
HEADS = WIDTH // HEAD_DIM
DECAY_RANK = 64
AAA_RANK = 64
GATE_RANK = 160
MV_RANK = 32
D_FF = 4 * D_MODEL
ALPHA = (2 * DEPTH) ** 0.25
LN_EPS = 1e-5
GN_EPS = 64e-5
OUT_NORM_EPS = 1e-6
DECAY_SCALE = math.exp(-0.5)
NEG_INF = -1e30
MASKED_KEY = 1e30
LOG2E = math.log2(math.e)
C_PARTS = 3
DENOM_LANE = (HEAD_DIM, HEAD_DIM - 1)
VT_ROWS = ((0, HEAD_DIM + 16), (HEAD_DIM - 16, LANES))

SM_F, SM_W, SM_A, SM_G, SM_V, SM_COLS = 0, 128, 256, 384, 640, 768

CHUNK = 64
SCAN_HEADS = 4
SCAN_W = SCAN_HEADS * HEAD_DIM
VMEM_LIMIT = 56 * 1024 * 1024


def _tile(total, cap, mult=64):
    best = None
    for d in range(mult, cap + 1, mult):
        if total % d == 0:
            best = d
    assert best is not None, (total, cap)
    return best


def _params(sem):
    return pltpu.CompilerParams(dimension_semantics=sem, vmem_limit_bytes=VMEM_LIMIT)


def _layer_norm(x, g, b):
    mean = jnp.mean(x, -1, keepdims=True)
    xc = x - mean
    var = jnp.mean(xc * xc, -1, keepdims=True)
    return xc * lax.rsqrt(var + LN_EPS) * g + b


def _ln_kernel(x_ref, g_ref, b_ref, of_ref, ob_ref):
    y = _layer_norm(x_ref[...], g_ref[...], b_ref[...])
    of_ref[...] = y
    ob_ref[...] = y.astype(BF16)


def _ln_rows(x, g, b, tp):
    n, d = x.shape
    tm = _tile(tp, 320)
    row = pl.BlockSpec((tm, d), lambda i: (i, 0))
    vec = pl.BlockSpec((1, d), lambda i: (0, 0))
    return pl.pallas_call(
        _ln_kernel,
        grid=(n // tm,),
        in_specs=[row, vec, vec],
        out_specs=[row, row],
        out_shape=[jax.ShapeDtypeStruct((n, d), F32), jax.ShapeDtypeStruct((n, d), BF16)],
        compiler_params=_params(("parallel",)),
        name="ln_in",
    )(x, g.reshape(1, d), b.reshape(1, d))


def _mm_kernel(x_ref, w_ref, o_ref):
    o_ref[...] = jnp.dot(x_ref[...], w_ref[...], preferred_element_type=F32).astype(o_ref.dtype)


def _matmul(x, w, out_dtype, tp, name):
    n, k = x.shape
    nc = w.shape[1]
    tm = _tile(tp, 640)
    tn = _tile(nc, 768, LANES)
    return pl.pallas_call(
        _mm_kernel,
        grid=(n // tm, nc // tn),
        in_specs=[pl.BlockSpec((tm, k), lambda i, j: (i, 0)),
                  pl.BlockSpec((k, tn), lambda i, j: (0, j))],
        out_specs=pl.BlockSpec((tm, tn), lambda i, j: (i, j)),
        out_shape=jax.ShapeDtypeStruct((n, nc), out_dtype),
        compiler_params=_params(("parallel", "parallel")),
        name=name,
    )(x, w)


def _cumsum_kernel(f_ref, fb_ref, o_ref, carry_ref, *, tb):
    t = pl.program_id(1)

    @pl.when(t == 0)
    def _():
        carry_ref[...] = jnp.zeros_like(carry_ref)

    z = f_ref[...] + fb_ref[...]
    lf = jnp.minimum(z, 0.0) - jnp.log1p(jnp.exp(-jnp.abs(z)))
    real = (t * tb + lax.broadcasted_iota(jnp.int32, (tb, LANES), 0)) >= FRONT_PAD
    lf = jnp.where(real, lf, 0.0)
    ri = lax.broadcasted_iota(jnp.int32, (tb, tb), 0)
    ci = lax.broadcasted_iota(jnp.int32, (tb, tb), 1)
    tri = (ci <= ri).astype(F32)
    c = jnp.dot(tri, lf, precision=HIGHEST, preferred_element_type=F32) + carry_ref[...]
    carry_ref[...] = c[tb - 1:tb, :]
    o_ref[...] = jnp.where(real, c, MASKED_KEY)


def _fox_cumsum(small, fb_pad, b, tp):
    tb = _tile(tp, 640, LANES)
    nblk = tp // tb
    blk = pl.BlockSpec((tb, LANES), lambda bi, t: (bi * nblk + t, 0))
    return pl.pallas_call(
        functools.partial(_cumsum_kernel, tb=tb),
        grid=(b, nblk),
        in_specs=[blk, pl.BlockSpec((1, LANES), lambda bi, t: (0, 0))],
        out_specs=blk,
        out_shape=jax.ShapeDtypeStruct((b * tp, LANES), F32),
        scratch_shapes=[pltpu.VMEM((1, LANES), F32)],
        compiler_params=_params(("parallel", "arbitrary")),
        name="fox_cumsum",
    )(small, fb_pad)


def _fox_prep_kernel(qkv_ref, c_ref, qo_ref, ko_ref, vo_ref, *, tm):
    lane = lax.broadcasted_iota(jnp.int32, (tm, LANES), 1)
    low = lane < HEAD_DIM
    cs = c_ref[...] * LOG2E
    hi = cs.astype(BF16).astype(F32)
    rest = cs - hi
    mid = rest.astype(BF16).astype(F32)
    lo = rest - mid
    for j in range(HEADS // 2):
        qb = qkv_ref[:, j * LANES:(j + 1) * LANES].astype(F32)
        kb = qkv_ref[:, WIDTH + j * LANES:WIDTH + (j + 1) * LANES].astype(F32)
        vb = qkv_ref[:, 2 * WIDTH + j * LANES:2 * WIDTH + (j + 1) * LANES].astype(F32)
        for hh in range(2):
            h = 2 * j + hh
            data = low if hh == 0 else jnp.logical_not(low)
            x0 = HEAD_DIM if hh == 0 else 0
            ones = jnp.where((lane >= x0) & (lane < x0 + C_PARTS), 1.0, 0.0)
            cpart = jnp.where(lane == x0, -hi[:, h:h + 1],
                              jnp.where(lane == x0 + 1, -mid[:, h:h + 1],
                                        jnp.where(lane == x0 + 2, -lo[:, h:h + 1], 0.0)))
            qo_ref[0, h] = jnp.where(data, qb, ones).astype(BF16)
            ko_ref[0, h] = jnp.where(data, kb, cpart).astype(BF16)
            vx = jnp.where(data, vb, jnp.where(lane == DENOM_LANE[hh], 1.0, 0.0))
            vo_ref[0, h] = vx.T.astype(BF16)


def _fox_prep(qkv, c, b, tp):
    tm = _tile(tp, 640, LANES)
    nt = tp // tm
    out = pl.BlockSpec((1, HEADS, tm, LANES), lambda bi, t: (bi, 0, t, 0))
    out_t = pl.BlockSpec((1, HEADS, LANES, tm), lambda bi, t: (bi, 0, 0, t))
    shape = jax.ShapeDtypeStruct((b, HEADS, tp, LANES), BF16)
    shape_t = jax.ShapeDtypeStruct((b, HEADS, LANES, tp), BF16)
    return pl.pallas_call(
        functools.partial(_fox_prep_kernel, tm=tm),
        grid=(b, nt),
        in_specs=[pl.BlockSpec((tm, 3 * WIDTH), lambda bi, t: (bi * nt + t, 0)),
                  pl.BlockSpec((tm, LANES), lambda bi, t: (bi * nt + t, 0))],
        out_specs=[out, out, out_t],
        out_shape=[shape, shape, shape_t],
        compiler_params=_params(("parallel", "parallel")),
        name="fox_prep",
    )(qkv, c)


def _fox_kernel(q_ref, k_ref, v_ref, g_ref, o_ref, m_sc, acc_sc, sa_sc, sb_sc, *, blk):
    qi = pl.program_id(2)
    ri = lax.broadcasted_iota(jnp.int32, (blk, blk), 0)
    ci = lax.broadcasted_iota(jnp.int32, (blk, blk), 1)
    causal = ri <= ci
    for h in range(2):
        m_sc[h] = jnp.full((1, blk), NEG_INF, F32)
        acc_sc[h] = jnp.zeros((LANES, blk), F32)

    def key_rows(j):
        return pl.ds(pl.multiple_of(j * blk, LANES), blk)

    def scores(j, s_sc):
        for h in range(2):
            s_sc[h] = lax.dot_general(k_ref[0, h, key_rows(j), :], q_ref[0, h],
                                      (((1,), (1,)), ((), ())), preferred_element_type=F32)

    def consume(j, s_sc, masked):
        for h in range(2):
            s = s_sc[h]
            if masked:
                s = jnp.where(causal, s, NEG_INF)
            lo, hi = VT_ROWS[h]
            m_prev = m_sc[h]
            m_next = jnp.maximum(m_prev, jnp.max(s, axis=0, keepdims=True))
            p = jnp.exp2(s - m_next)
            acc_sc[h, lo:hi, :] = (jnp.exp2(m_prev - m_next) * acc_sc[h, lo:hi, :]
                                   + jnp.dot(v_ref[0, h, lo:hi, key_rows(j)], p.astype(BF16),
                                             preferred_element_type=F32))
            m_sc[h] = m_next

    scores(0, sa_sc)

    def pair(p, carry):
        j = 2 * p
        scores(j + 1, sb_sc)
        consume(j, sa_sc, False)
        scores(j + 2, sa_sc)
        consume(j + 1, sb_sc, False)
        return carry

    lax.fori_loop(0, qi // 2, pair, 0)

    @pl.when(qi % 2 == 1)
    def _():
        scores(qi, sb_sc)
        consume(qi - 1, sa_sc, False)
        consume(qi, sb_sc, True)

    @pl.when(qi % 2 == 0)
    def _():
        consume(qi, sa_sc, True)

    lane = lax.broadcasted_iota(jnp.int32, (blk, LANES), 1)
    low = lane < HEAD_DIM
    outs = []
    for h in range(2):
        data = low if h == 0 else jnp.logical_not(low)
        acc = acc_sc[h].T
        o = jnp.where(data, acc / acc[:, DENOM_LANE[h]:DENOM_LANE[h] + 1], 0.0)
        ms = jnp.sum(o * o, axis=1, keepdims=True) * (1.0 / HEAD_DIM)
        outs.append(o * lax.rsqrt(ms + OUT_NORM_EPS))
    o_ref[...] = ((outs[0] + outs[1]) * g_ref[0]).astype(o_ref.dtype)


def _fox_attention(qp, kp, vt, gains, b, tp):
    blk = _tile(tp, 640, LANES)
    nq = tp // blk
    hp = HEADS // 2
    kern = functools.partial(_fox_kernel, blk=blk)
    return pl.pallas_call(
        kern,
        grid=(b, hp, nq),
        in_specs=[pl.BlockSpec((1, 2, blk, LANES), lambda bi, h, qi: (bi, h, qi, 0)),
                  pl.BlockSpec((1, 2, tp, LANES), lambda bi, h, qi: (bi, h, 0, 0)),
                  pl.BlockSpec((1, 2, LANES, tp), lambda bi, h, qi: (bi, h, 0, 0)),
                  pl.BlockSpec((1, 1, LANES), lambda bi, h, qi: (h, 0, 0))],
        out_specs=pl.BlockSpec((blk, LANES), lambda bi, h, qi: (bi * nq + qi, h)),
        out_shape=jax.ShapeDtypeStruct((b * tp, WIDTH), BF16),
        scratch_shapes=[pltpu.VMEM((2, 1, blk), F32), pltpu.VMEM((2, LANES, blk), F32),
                        pltpu.VMEM((2, blk, blk), F32), pltpu.VMEM((2, blk, blk), F32)],
        compiler_params=_params(("parallel", "parallel", "arbitrary")),
        name="fox_attention",
    )(qp, kp, vt, gains)


def _split_dot(x, sel, parts):
    out = None
    for _ in range(parts):
        piece = x.astype(BF16)
        term = jnp.dot(piece, sel, preferred_element_type=F32)
        out = term if out is None else out + term
        x = x - piece.astype(F32)
    return out


def _head_sum(x, sel_ref, selt_ref):
    return _split_dot(_split_dot(x, sel_ref[...], 2), selt_ref[...], 2)


def _prep_kernel(*refs, tm, tiles_per_batch, first):
    if first:
        (big_ref, sm_ref, mub_ref, mus_ref, wup_ref, w0_ref, aup_ref, a0_ref, gup_ref,
         kk_ref, ka_ref, rk_ref, sel_ref, selt_ref,
         r_o, lw_o, k_o, v_o, kn_o, a_o, g_o, bg_o, vf_o, cb_sc, cs_sc) = refs
    else:
        (big_ref, sm_ref, mub_ref, mus_ref, wup_ref, w0_ref, aup_ref, a0_ref, gup_ref,
         vup_ref, v0_ref, vf_ref, kk_ref, ka_ref, rk_ref, sel_ref, selt_ref,
         r_o, lw_o, k_o, v_o, kn_o, a_o, g_o, bg_o, cb_sc, cs_sc) = refs
    i = pl.program_id(0)

    @pl.when(i == 0)
    def _():
        cb_sc[...] = jnp.zeros_like(cb_sc)
        cs_sc[...] = jnp.zeros_like(cs_sc)

    ri = lax.broadcasted_iota(jnp.int32, (tm, 1), 0)
    real = (lax.rem(i, tiles_per_batch) * tm + ri) >= FRONT_PAD

    def mix(x_ref, mu_ref, carry):
        h = jnp.where(real, x_ref[...], 0.0)
        prev = jnp.where(ri == 0, carry[...], pltpu.roll(h, 1, axis=0))
        carry[...] = h[tm - 1:tm, :]
        p = h + mu_ref[...] * (prev - h)
        return jnp.where(real, p, 0.0)

    pb = mix(big_ref, mub_ref, cb_sc)
    ps = mix(sm_ref, mus_ref, cs_sc)
    r = pb[:, 0:WIDTH]
    k = pb[:, WIDTH:2 * WIDTH]
    v = pb[:, 2 * WIDTH:3 * WIDTH]

    def lora(x, w_ref):
        return jnp.dot(x.astype(BF16), w_ref[...], preferred_element_type=F32)

    lw = -DECAY_SCALE * jax.nn.sigmoid(w0_ref[...] + lora(jnp.tanh(ps[:, SM_W:SM_A]), wup_ref))
    a = jax.nn.sigmoid(a0_ref[...] + lora(ps[:, SM_A:SM_G], aup_ref))
    g = lora(jax.nn.sigmoid(ps[:, SM_G:SM_V]), gup_ref)
    if first:
        vf_o[...] = v
    else:
        v = v + (vf_ref[...] - v) * jax.nn.sigmoid(v0_ref[...] + lora(ps[:, SM_V:SM_COLS], vup_ref))

    kn = k * kk_ref[...]
    ss = _split_dot(kn * kn, sel_ref[...], 2)
    inv = 1.0 / jnp.maximum(jnp.sqrt(ss), 1e-12)
    kn = kn * _split_dot(inv, selt_ref[...], 2)
    k = k * (1.0 + (a - 1.0) * ka_ref[...])
    bonus = _head_sum(r * k * rk_ref[...], sel_ref, selt_ref) * v

    r_o[...] = r
    lw_o[...] = lw
    k_o[...] = k
    v_o[...] = v
    kn_o[...] = kn
    a_o[...] = a
    g_o[...] = g
    bg_o[...] = bonus * g


def _rwkv_prep(big, small, lp, sel, selt, v_first, tp):
    n = big.shape[0]
    first = v_first is None
    tm = _tile(tp, 320)
    row = lambda c: pl.BlockSpec((tm, c), lambda i: (i, 0))
    full = lambda a: pl.BlockSpec(a.shape, lambda i: (0,) * a.ndim)
    ins = [big, small, lp["mu_big"], lp["mu_small"], lp["w_up"], lp["w0"], lp["a_up"], lp["a0"], lp["g_up"]]
    specs = [row(3 * WIDTH), row(SM_COLS)] + [full(a) for a in ins[2:]]
    if not first:
        ins += [lp["v_up"], lp["v0"], v_first]
        specs += [full(lp["v_up"]), full(lp["v0"]), row(WIDTH)]
    tail = [lp["k_k"], lp["k_a"], lp["r_k"], sel, selt]
    ins += tail
    specs += [full(a) for a in tail]
    n_out = 9 if first else 8
    kern = functools.partial(_prep_kernel, tm=tm, tiles_per_batch=tp // tm, first=first)
    return pl.pallas_call(
        kern,
        grid=(n // tm,),
        in_specs=specs,
        out_specs=[row(WIDTH)] * n_out,
        out_shape=[jax.ShapeDtypeStruct((n, WIDTH), F32)] * n_out,
        scratch_shapes=[pltpu.VMEM((1, 3 * WIDTH), F32), pltpu.VMEM((1, SM_COLS), F32)],
        compiler_params=_params(("arbitrary",)),
        name="rwkv_prep",
    )(*ins)


def _scan_kernel(r_ref, lw_ref, k_ref, v_ref, kn_ref, a_ref, y_ref, s_sc, *, tt):
    ti = pl.program_id(1)

    @pl.when(ti == 0)
    def _():
        s_sc[...] = jnp.zeros_like(s_sc)

    w = SCAN_W
    ri = lax.broadcasted_iota(jnp.int32, (w, w), 0)
    ci = lax.broadcasted_iota(jnp.int32, (w, w), 1)
    same_head = (ri // HEAD_DIM) == (ci // HEAD_DIM)
    strict = same_head & (ci < ri)
    incl = same_head & (ci <= ri)
    eye = (ri == ci).astype(F32)
    tri_r = lax.broadcasted_iota(jnp.int32, (CHUNK, CHUNK), 0)
    tri_c = lax.broadcasted_iota(jnp.int32, (CHUNK, CHUNK), 1)
    tri = (tri_c <= tri_r).astype(BF16)

    def stack(x):
        return jnp.where(same_head, jnp.tile(x, (SCAN_HEADS, 1)), 0.0).astype(BF16)

    def mm(x, y):
        return jnp.dot(x, y, preferred_element_type=F32)

    def mm_nt(x, y):
        return lax.dot_general(x, y, (((1,), (1,)), ((), ())), preferred_element_type=F32)

    def mm_tn(x, y):
        return lax.dot_general(x, y, (((0,), (0,)), ((), ())), preferred_element_type=F32)

    groups = range(WIDTH // w)

    def prepare(rows, gi):
        cols = slice(gi * w, (gi + 1) * w)
        r, lw, k, v = r_ref[rows, cols], lw_ref[rows, cols], k_ref[rows, cols], v_ref[rows, cols]
        kn, a = kn_ref[rows, cols], a_ref[rows, cols]
        cum = None
        rest = lw
        for _ in range(3):
            piece = rest.astype(BF16)
            term = jnp.dot(tri, piece, preferred_element_type=F32)
            cum = term if cum is None else cum + term
            rest = rest - piece.astype(F32)
        p_in = jnp.exp(cum)
        p_ex = jnp.exp(cum - lw)
        p_inv = jnp.exp(-cum)
        p_end = p_in[CHUNK - 1:CHUNK, :]
        bt = kn * a * p_inv
        kt = k * p_inv
        ar = jnp.concatenate([stack(-kn * p_ex), stack(r * p_in)], axis=0)
        bk = jnp.concatenate([stack(bt), stack(kt)], axis=0)
        return dict(ar=ar, bk=bk, vs=stack(v), p_end=p_end,
                    bh=stack(bt * p_end), kh=stack(kt * p_end))

    def chunk(c, carry):
        rows = pl.ds(pl.multiple_of(c * CHUNK, CHUNK), CHUNK)
        g = [prepare(rows, gi) for gi in groups]
        gram = [mm_nt(x["ar"], x["bk"]) for x in g]
        a_ab = [jnp.where(strict, m[0:w, 0:w], 0.0) for m in gram]
        a_kv = [jnp.concatenate([jnp.where(strict, m[0:w, w:2 * w], 0.0),
                                 jnp.where(incl, m[w:2 * w, w:2 * w], 0.0)], axis=0).astype(BF16)
                for m in gram]
        a_rb = [jnp.where(incl, m[w:2 * w, 0:w], 0.0).astype(BF16) for m in gram]
        s0 = [s_sc[gi] for gi in groups]
        xr = [mm_nt(x["ar"], s.astype(BF16)) for x, s in zip(g, s0)]
        av = [mm(m, x["vs"]) for m, x in zip(a_kv, g)]
        inv = [eye + m for m in a_ab]
        apow = [m.astype(BF16) for m in a_ab]
        for _ in range(int(math.log2(CHUNK)) - 1):
            apow = [mm(m, m).astype(BF16) for m in apow]
            inv = [t + mm(t.astype(BF16), m) for t, m in zip(inv, apow)]
        ub = [mm(t.astype(BF16), (x_[0:w] + v_[0:w]).astype(BF16)).astype(BF16)
              for t, x_, v_ in zip(inv, xr, av)]
        y4 = [x_[w:2 * w] + mm(m, u_) + v_[w:2 * w] for x_, m, u_, v_ in zip(xr, a_rb, ub, av)]
        for gi in groups:
            y = y4[gi][0:CHUNK]
            for h in range(1, SCAN_HEADS):
                y = y + y4[gi][h * CHUNK:(h + 1) * CHUNK]
            y_ref[rows, gi * w:(gi + 1) * w] = y
        for gi in groups:
            s_sc[gi] = (s0[gi] * g[gi]["p_end"] + mm_tn(ub[gi], g[gi]["bh"])
                        + mm_tn(g[gi]["vs"], g[gi]["kh"]))
        return carry

    lax.fori_loop(0, tt // CHUNK, chunk, 0)


def _rwkv_scan(r, lw, k, v, kn, a, b, tp):
    n = r.shape[0]
    tt = _tile(tp, 320)
    nt = tp // tt
    spec = pl.BlockSpec((tt, WIDTH), lambda bi, ti: (bi * nt + ti, 0))
    kern = functools.partial(_scan_kernel, tt=tt)
    return pl.pallas_call(
        kern,
        grid=(b, nt),
        in_specs=[spec] * 6,
        out_specs=spec,
        out_shape=jax.ShapeDtypeStruct((n, WIDTH), F32),
        scratch_shapes=[pltpu.VMEM((WIDTH // SCAN_W, SCAN_W, SCAN_W), F32)],
        compiler_params=_params(("parallel", "arbitrary")),
        name="rwkv_scan",
    )(r, lw, k, v, kn, a)


def _post_kernel(y_ref, g_ref, bg_ref, lg_ref, lb_ref, sel_ref, selt_ref, o_ref):
    y = y_ref[...]
    mean = _head_sum(y, sel_ref, selt_ref) * (1.0 / HEAD_DIM)
    yc = y - mean
    var = _head_sum(yc * yc, sel_ref, selt_ref) * (1.0 / HEAD_DIM)
    yn = yc * lax.rsqrt(var + GN_EPS) * lg_ref[...] + lb_ref[...]
    o_ref[...] = (yn * g_ref[...] + bg_ref[...]).astype(o_ref.dtype)


def _rwkv_post(y, g, bg, lnx_g, lnx_b, sel, selt, tp):
    n = y.shape[0]
    tm = _tile(tp, 320)
    row = pl.BlockSpec((tm, WIDTH), lambda i: (i, 0))
    full = lambda a: pl.BlockSpec(a.shape, lambda i: (0,) * a.ndim)
    return pl.pallas_call(
        _post_kernel,
        grid=(n // tm,),
        in_specs=[row, row, row, full(lnx_g), full(lnx_b), full(sel), full(selt)],
        out_specs=row,
        out_shape=jax.ShapeDtypeStruct((n, WIDTH), BF16),
        compiler_params=_params(("parallel",)),
        name="rwkv_post",
    )(y, g, bg, lnx_g, lnx_b, sel, selt)


def _outproj_kernel(fox_ref, rwkv_ref, h_ref, wt_ref, wb_ref, g_ref, b_ref, of_ref, ob_ref):
    mixed = (jnp.dot(fox_ref[...], wt_ref[...], preferred_element_type=F32)
             + jnp.dot(rwkv_ref[...], wb_ref[...], preferred_element_type=F32))
    y = _layer_norm(ALPHA * h_ref[...] + mixed, g_ref[...], b_ref[...])
    of_ref[...] = y
    ob_ref[...] = y.astype(BF16)


def _outproj_ln(fox, rwkv, h, w_top, w_bot, g, b, tp):
    n, d = h.shape
    tm = _tile(tp, 320)
    half = pl.BlockSpec((tm, WIDTH), lambda i: (i, 0))
    row = pl.BlockSpec((tm, d), lambda i: (i, 0))
    wsp = pl.BlockSpec((WIDTH, d), lambda i: (0, 0))
    vec = pl.BlockSpec((1, d), lambda i: (0, 0))
    return pl.pallas_call(
        _outproj_kernel,
        grid=(n // tm,),
        in_specs=[half, half, row, wsp, wsp, vec, vec],
        out_specs=[row, row],
        out_shape=[jax.ShapeDtypeStruct((n, d), F32), jax.ShapeDtypeStruct((n, d), BF16)],
        compiler_params=_params(("parallel",)),
        name="outproj_ln",
    )(fox, rwkv, h, w_top, w_bot, g, b)


def _ffn_kernel(xb_ref, xf_ref, w1_ref, w2_ref, g_ref, b_ref, of_ref, ob_ref, acc_sc):
    j = pl.program_id(1)

    @pl.when(j == 0)
    def _():
        acc_sc[...] = jnp.zeros_like(acc_sc)

    u = jnp.maximum(jnp.dot(xb_ref[...], w1_ref[...], preferred_element_type=F32), 0.0)
    acc_sc[...] += jnp.dot((u * u).astype(BF16), w2_ref[...], preferred_element_type=F32)

    @pl.when(j == pl.num_programs(1) - 1)
    def _():
        y = _layer_norm(ALPHA * xf_ref[...] + acc_sc[...], g_ref[...], b_ref[...])
        of_ref[...] = y
        ob_ref[...] = y.astype(BF16)


def _ffn_ln(xb, xf, w1, w2, g, b, tp):
    n, d = xf.shape
    tm = _tile(tp, 640)
    tf = 512
    row = pl.BlockSpec((tm, d), lambda i, j: (i, 0))
    vec = pl.BlockSpec((1, d), lambda i, j: (0, 0))
    return pl.pallas_call(
        _ffn_kernel,
        grid=(n // tm, D_FF // tf),
        in_specs=[row, row,
                  pl.BlockSpec((d, tf), lambda i, j: (0, j)),
                  pl.BlockSpec((tf, d), lambda i, j: (j, 0)),
                  vec, vec],
        out_specs=[row, row],
        out_shape=[jax.ShapeDtypeStruct((n, d), F32), jax.ShapeDtypeStruct((n, d), BF16)],
        scratch_shapes=[pltpu.VMEM((tm, d), F32)],
        compiler_params=_params(("parallel", "arbitrary")),
        name="ffn_ln",
    )(xb, xf, w1, w2, g, b)


def _pad_cols(a, width):
    return jnp.pad(a, ((0, 0), (0, width - a.shape[1])))


def _pad_rows(a, height):
    return jnp.pad(a, ((0, height - a.shape[0]), (0, 0)))


def _layer_params(l, w_in, mu, fox_fb, fox_out_g, w_up, w0, a_up, a0, g_up, v_up, v0,
                  k_k, k_a, r_k, lnx_g, lnx_b, w_out, ln1_g, ln1_b, w_ff1, w_ff2, ln2_g, ln2_b):
    first = v_up is None
    fq, fk, fv, ff = 0, WIDTH, 2 * WIDTH, 3 * WIDTH
    rs = 3 * WIDTH + HEADS
    o_w = 3 * WIDTH
    o_a = o_w + DECAY_RANK
    o_g = o_a + AAA_RANK
    o_v = o_g + GATE_RANK
    scale = HEAD_DIM ** -0.5 * LOG2E
    w_fox = jnp.concatenate([w_in[:, fq:fk] * scale, w_in[:, fk:ff]], axis=1).astype(BF16)
    wr = w_in[:, rs:]
    w_big = wr[:, 0:o_w].astype(BF16)
    smalls = [(w_in[:, ff:rs], SM_W - SM_F), (wr[:, o_w:o_a], SM_A - SM_W), (wr[:, o_a:o_g], SM_G - SM_A),
              (wr[:, o_g:o_v], SM_V - SM_G)]
    mus = [(jnp.zeros((1, HEADS), F32), SM_W - SM_F), (mu[None, o_w:o_a], SM_A - SM_W),
           (mu[None, o_a:o_g], SM_G - SM_A), (mu[None, o_g:o_v], SM_V - SM_G)]
    if first:
        smalls.append((jnp.zeros((D_MODEL, 0), F32), SM_COLS - SM_V))
        mus.append((jnp.zeros((1, 0), F32), SM_COLS - SM_V))
    else:
        smalls.append((wr[:, o_v:], SM_COLS - SM_V))
        mus.append((mu[None, o_v:], SM_COLS - SM_V))
    w_small = jnp.concatenate([_pad_cols(a, wd) for a, wd in smalls], axis=1).astype(BF16)
    mu_small = jnp.concatenate([_pad_cols(a, wd) for a, wd in mus], axis=1)
    row = lambda a: a.reshape(1, -1)
    lp = dict(
        w_fox=w_fox, w_big=w_big, w_small=w_small,
        mu_big=mu[None, 0:o_w], mu_small=mu_small,
        fb=_pad_cols(fox_fb[None, :], LANES),
        gains=fox_out_g.reshape(HEADS // 2, 1, 2 * HEAD_DIM),
        w_up=_pad_rows(w_up, SM_A - SM_W).astype(BF16), w0=row(w0),
        a_up=_pad_rows(a_up, SM_G - SM_A).astype(BF16), a0=row(a0),
        g_up=_pad_rows(g_up, SM_V - SM_G).astype(BF16),
        k_k=row(k_k), k_a=row(k_a), r_k=row(r_k), lnx_g=row(lnx_g), lnx_b=row(lnx_b),
        w_top=w_out[:WIDTH].astype(BF16), w_bot=w_out[WIDTH:].astype(BF16),
        ln1_g=row(ln1_g), ln1_b=row(ln1_b),
        w_ff1=w_ff1.astype(BF16), w_ff2=w_ff2.astype(BF16),
        ln2_g=row(ln2_g), ln2_b=row(ln2_b),
    )
    if not first:
        lp["v_up"] = _pad_rows(v_up, SM_COLS - SM_V).astype(BF16)
        lp["v0"] = row(v0)
    return lp


def _layer(hf, hb, lp, sel, selt, v_first, b, tp):
    qkv = _matmul(hb, lp["w_fox"], BF16, tp, "inproj_fox")
    big = _matmul(hb, lp["w_big"], F32, tp, "inproj_rwkv")
    small = _matmul(hb, lp["w_small"], F32, tp, "inproj_small")
    c = _fox_cumsum(small, lp["fb"], b, tp)
    qp, kp, vt = _fox_prep(qkv, c, b, tp)
    fox = _fox_attention(qp, kp, vt, lp["gains"], b, tp)
    outs = _rwkv_prep(big, small, lp, sel, selt, v_first, tp)
    r, lw, k, v, kn, a, g, bg = outs[:8]
    if v_first is None:
        v_first = outs[8]
    y = _rwkv_scan(r, lw, k, v, kn, a, b, tp)
    rwkv = _rwkv_post(y, g, bg, lp["lnx_g"], lp["lnx_b"], sel, selt, tp)
    xf, xb = _outproj_ln(fox, rwkv, hf, lp["w_top"], lp["w_bot"], lp["ln1_g"], lp["ln1_b"], tp)
    hf, hb = _ffn_ln(xb, xf, lp["w_ff1"], lp["w_ff2"], lp["ln2_g"], lp["ln2_b"], tp)
    return hf, hb, v_first


def kernel(x, meta, ln_in_g, ln_in_b, w_in_first, w_in_rest, mu_first, mu_rest, fox_fb, fox_out_g, w_up, w0, a_up, a0, g_up, v_up, v0, k_k, k_a, r_k, lnx_g, lnx_b, w_out, ln1_g, ln1_b, w_ff1, w_ff2, ln2_g, ln2_b):
    b, seq, d = x.shape
    assert d == D_MODEL
    tp = FRONT_PAD + N_META + seq
    assert tp % LANES == 0
    head = jnp.concatenate([jnp.zeros((FRONT_PAD, d), x.dtype), meta.astype(x.dtype)], axis=0)
    h0 = jnp.concatenate([jnp.broadcast_to(head[None], (b, LANES, d)), x], axis=1).reshape(b * tp, d)
    hf, hb = _ln_rows(h0, ln_in_g, ln_in_b, tp)

    lane_head = jnp.arange(WIDTH, dtype=jnp.int32) // HEAD_DIM
    sel = (lane_head[:, None] == jnp.arange(LANES, dtype=jnp.int32)[None, :]).astype(BF16)
    selt = sel.T

    v_first = None
    for l in range(DEPTH):
        first = l == 0
        lp = _layer_params(
            l, w_in_first if first else w_in_rest[l - 1], mu_first if first else mu_rest[l - 1],
            fox_fb[l], fox_out_g[l], w_up[l], w0[l], a_up[l], a0[l], g_up[l],
            None if first else v_up[l - 1], None if first else v0[l - 1],
            k_k[l], k_a[l], r_k[l], lnx_g[l], lnx_b[l], w_out[l],
            ln1_g[l], ln1_b[l], w_ff1[l], w_ff2[l], ln2_g[l], ln2_b[l])
        hf, hb, v_first = _layer(hf, hb, lp, sel, selt, v_first, b, tp)
    return hf.reshape(b, tp, d)[:, LANES:, :]
```

```python
import functools
import math

import jax
import jax.numpy as jnp
from jax import lax
from jax.experimental import pallas as pl
from jax.experimental.pallas import tpu as pltpu

F32 = jnp.float32
BF16 = jnp.bfloat16
HIGHEST = lax.Precision.HIGHEST

D_MODEL = 2048
DEPTH = 4
N_META = 16
LANES = 128
FRONT_PAD = (-N_META) % LANES
HEAD_DIM = 64
WIDTH = D_MODEL // 2
HEADS = WIDTH // HEAD_DIM
DECAY_RANK = 64
AAA_RANK = 64
GATE_RANK = 160
MV_RANK = 32
D_FF = 4 * D_MODEL
ALPHA = (2 * DEPTH) ** 0.25
LN_EPS = 1e-5
GN_EPS = 64e-5
OUT_NORM_EPS = 1e-6
DECAY_SCALE = math.exp(-0.5)
NEG_INF = -1e30
MASKED_KEY = 1e30
LOG2E = math.log2(math.e)
C_PARTS = 3
DENOM_LANE = (HEAD_DIM, HEAD_DIM - 1)
VT_ROWS = ((0, HEAD_DIM + 16), (HEAD_DIM - 16, LANES))

SM_F, SM_W, SM_A, SM_G, SM_V, SM_COLS = 0, 128, 256, 384, 640, 768

CHUNK = 64
SCAN_HEADS = 4
SCAN_W = SCAN_HEADS * HEAD_DIM
VMEM_LIMIT = 56 * 1024 * 1024


def _tile(total, cap, mult=64):
    best = None
    for d in range(mult, cap + 1, mult):
        if total % d == 0:
            best = d
    assert best is not None, (total, cap)
    return best


def _params(sem):
    return pltpu.CompilerParams(dimension_semantics=sem, vmem_limit_bytes=VMEM_LIMIT)


def _layer_norm(x, g, b):
    mean = jnp.mean(x, -1, keepdims=True)
    xc = x - mean
    var = jnp.mean(xc * xc, -1, keepdims=True)
    return xc * lax.rsqrt(var + LN_EPS) * g + b


def _ln_kernel(x_ref, g_ref, b_ref, of_ref, ob_ref):
    y = _layer_norm(x_ref[...], g_ref[...], b_ref[...])
    of_ref[...] = y
    ob_ref[...] = y.astype(BF16)


def _ln_rows(x, g, b, tp):
    n, d = x.shape
    tm = _tile(tp, 320)
    row = pl.BlockSpec((tm, d), lambda i: (i, 0))
    vec = pl.BlockSpec((1, d), lambda i: (0, 0))
    return pl.pallas_call(
        _ln_kernel,
        grid=(n // tm,),
        in_specs=[row, vec, vec],
        out_specs=[row, row],
        out_shape=[jax.ShapeDtypeStruct((n, d), F32), jax.ShapeDtypeStruct((n, d), BF16)],
        compiler_params=_params(("parallel",)),
        name="ln_in",
    )(x, g.reshape(1, d), b.reshape(1, d))


def _mm_kernel(x_ref, w_ref, o_ref):
    o_ref[...] = jnp.dot(x_ref[...], w_ref[...], preferred_element_type=F32).astype(o_ref.dtype)


def _matmul(x, w, out_dtype, name):
    n, k = x.shape
    nc = w.shape[1]
    tm = _tile(n, 1664)
    tn = _tile(nc, 768, LANES)
    return pl.pallas_call(
        _mm_kernel,
        grid=(n // tm, nc // tn),
        in_specs=[pl.BlockSpec((tm, k), lambda i, j: (i, 0)),
                  pl.BlockSpec((k, tn), lambda i, j: (0, j))],
        out_specs=pl.BlockSpec((tm, tn), lambda i, j: (i, j)),
        out_shape=jax.ShapeDtypeStruct((n, nc), out_dtype),
        compiler_params=_params(("parallel", "parallel")),
        name=name,
    )(x, w)


def _cumsum_kernel(f_ref, fb_ref, o_ref, carry_ref, *, tb):
    t = pl.program_id(1)

    @pl.when(t == 0)
    def _():
        carry_ref[...] = jnp.zeros_like(carry_ref)

    z = f_ref[...] + fb_ref[...]
    lf = jnp.minimum(z, 0.0) - jnp.log1p(jnp.exp(-jnp.abs(z)))
    real = (t * tb + lax.broadcasted_iota(jnp.int32, (tb, LANES), 0)) >= FRONT_PAD
    lf = jnp.where(real, lf, 0.0)
    ri = lax.broadcasted_iota(jnp.int32, (tb, tb), 0)
    ci = lax.broadcasted_iota(jnp.int32, (tb, tb), 1)
    tri = (ci <= ri).astype(F32)
    c = jnp.dot(tri, lf, precision=HIGHEST, preferred_element_type=F32) + carry_ref[...]
    carry_ref[...] = c[tb - 1:tb, :]
    o_ref[...] = jnp.where(real, c, MASKED_KEY)


def _fox_cumsum(small, fb_pad, b, tp):
    tb = _tile(tp, 640, LANES)
    nblk = tp // tb
    blk = pl.BlockSpec((tb, LANES), lambda bi, t: (bi * nblk + t, 0))
    return pl.pallas_call(
        functools.partial(_cumsum_kernel, tb=tb),
        grid=(b, nblk),
        in_specs=[blk, pl.BlockSpec((1, LANES), lambda bi, t: (0, 0))],
        out_specs=blk,
        out_shape=jax.ShapeDtypeStruct((b * tp, LANES), F32),
        scratch_shapes=[pltpu.VMEM((1, LANES), F32)],
        compiler_params=_params(("parallel", "arbitrary")),
        name="fox_cumsum",
    )(small, fb_pad)


def _fox_prep_kernel(qkv_ref, c_ref, qo_ref, ko_ref, vo_ref, *, tm):
    lane = lax.broadcasted_iota(jnp.int32, (tm, LANES), 1)
    low = lane < HEAD_DIM
    cs = c_ref[...] * LOG2E
    hi = cs.astype(BF16).astype(F32)
    rest = cs - hi
    mid = rest.astype(BF16).astype(F32)
    lo = rest - mid
    for j in range(HEADS // 2):
        qb = qkv_ref[:, j * LANES:(j + 1) * LANES].astype(F32)
        kb = qkv_ref[:, WIDTH + j * LANES:WIDTH + (j + 1) * LANES].astype(F32)
        vb = qkv_ref[:, 2 * WIDTH + j * LANES:2 * WIDTH + (j + 1) * LANES].astype(F32)
        for hh in range(2):
            h = 2 * j + hh
            data = low if hh == 0 else jnp.logical_not(low)
            x0 = HEAD_DIM if hh == 0 else 0
            ones = jnp.where((lane >= x0) & (lane < x0 + C_PARTS), 1.0, 0.0)
            cpart = jnp.where(lane == x0, -hi[:, h:h + 1],
                              jnp.where(lane == x0 + 1, -mid[:, h:h + 1],
                                        jnp.where(lane == x0 + 2, -lo[:, h:h + 1], 0.0)))
            qo_ref[0, h] = jnp.where(data, qb, ones).astype(BF16)
            ko_ref[0, h] = jnp.where(data, kb, cpart).astype(BF16)
            vx = jnp.where(data, vb, jnp.where(lane == DENOM_LANE[hh], 1.0, 0.0))
            vo_ref[0, h] = vx.T.astype(BF16)


def _fox_prep(qkv, c, b, tp):
    tm = _tile(tp, 640, LANES)
    nt = tp // tm
    out = pl.BlockSpec((1, HEADS, tm, LANES), lambda bi, t: (bi, 0, t, 0))
    out_t = pl.BlockSpec((1, HEADS, LANES, tm), lambda bi, t: (bi, 0, 0, t))
    shape = jax.ShapeDtypeStruct((b, HEADS, tp, LANES), BF16)
    shape_t = jax.ShapeDtypeStruct((b, HEADS, LANES, tp), BF16)
    return pl.pallas_call(
        functools.partial(_fox_prep_kernel, tm=tm),
        grid=(b, nt),
        in_specs=[pl.BlockSpec((tm, 3 * WIDTH), lambda bi, t: (bi * nt + t, 0)),
                  pl.BlockSpec((tm, LANES), lambda bi, t: (bi * nt + t, 0))],
        out_specs=[out, out, out_t],
        out_shape=[shape, shape, shape_t],
        compiler_params=_params(("parallel", "parallel")),
        name="fox_prep",
    )(qkv, c)


def _fox_kernel(q_ref, k_ref, v_ref, g_ref, o_ref, m_sc, acc_sc, sa_sc, sb_sc, *, blk):
    qi = pl.program_id(2)
    ri = lax.broadcasted_iota(jnp.int32, (blk, blk), 0)
    ci = lax.broadcasted_iota(jnp.int32, (blk, blk), 1)
    causal = ri <= ci
    for h in range(2):
        m_sc[h] = jnp.full((1, blk), NEG_INF, F32)
        acc_sc[h] = jnp.zeros((LANES, blk), F32)

    def key_rows(j):
        return pl.ds(pl.multiple_of(j * blk, LANES), blk)

    def scores(j, s_sc):
        for h in range(2):
            s_sc[h] = lax.dot_general(k_ref[0, h, key_rows(j), :], q_ref[0, h],
                                      (((1,), (1,)), ((), ())), preferred_element_type=F32)

    def consume(j, s_sc, masked):
        for h in range(2):
            s = s_sc[h]
            if masked:
                s = jnp.where(causal, s, NEG_INF)
            lo, hi = VT_ROWS[h]
            m_prev = m_sc[h]
            m_next = jnp.maximum(m_prev, jnp.max(s, axis=0, keepdims=True))
            p = jnp.exp2(s - m_next)
            acc_sc[h, lo:hi, :] = (jnp.exp2(m_prev - m_next) * acc_sc[h, lo:hi, :]
                                   + jnp.dot(v_ref[0, h, lo:hi, key_rows(j)], p.astype(BF16),
                                             preferred_element_type=F32))
            m_sc[h] = m_next

    scores(0, sa_sc)

    def pair(p, carry):
        j = 2 * p
        scores(j + 1, sb_sc)
        consume(j, sa_sc, False)
        scores(j + 2, sa_sc)
        consume(j + 1, sb_sc, False)
        return carry

    lax.fori_loop(0, qi // 2, pair, 0)

    @pl.when(qi % 2 == 1)
    def _():
        scores(qi, sb_sc)
        consume(qi - 1, sa_sc, False)
        consume(qi, sb_sc, True)

    @pl.when(qi % 2 == 0)
    def _():
        consume(qi, sa_sc, True)

    lane = lax.broadcasted_iota(jnp.int32, (blk, LANES), 1)
    low = lane < HEAD_DIM
    outs = []
    for h in range(2):
        data = low if h == 0 else jnp.logical_not(low)
        acc = acc_sc[h].T
        o = jnp.where(data, acc / acc[:, DENOM_LANE[h]:DENOM_LANE[h] + 1], 0.0)
        ms = jnp.sum(o * o, axis=1, keepdims=True) * (1.0 / HEAD_DIM)
        outs.append(o * lax.rsqrt(ms + OUT_NORM_EPS))
    o_ref[...] = ((outs[0] + outs[1]) * g_ref[0]).astype(o_ref.dtype)


def _fox_attention(qp, kp, vt, gains, b, tp):
    blk = _tile(tp, 640, LANES)
    nq = tp // blk
    hp = HEADS // 2
    kern = functools.partial(_fox_kernel, blk=blk)
    return pl.pallas_call(
        kern,
        grid=(b, hp, nq),
        in_specs=[pl.BlockSpec((1, 2, blk, LANES), lambda bi, h, qi: (bi, h, qi, 0)),
                  pl.BlockSpec((1, 2, tp, LANES), lambda bi, h, qi: (bi, h, 0, 0)),
                  pl.BlockSpec((1, 2, LANES, tp), lambda bi, h, qi: (bi, h, 0, 0)),
                  pl.BlockSpec((1, 1, LANES), lambda bi, h, qi: (h, 0, 0))],
        out_specs=pl.BlockSpec((blk, LANES), lambda bi, h, qi: (bi * nq + qi, h)),
        out_shape=jax.ShapeDtypeStruct((b * tp, WIDTH), BF16),
        scratch_shapes=[pltpu.VMEM((2, 1, blk), F32), pltpu.VMEM((2, LANES, blk), F32),
                        pltpu.VMEM((2, blk, blk), F32), pltpu.VMEM((2, blk, blk), F32)],
        compiler_params=_params(("parallel", "parallel", "arbitrary")),
        name="fox_attention",
    )(qp, kp, vt, gains)


def _split_dot(x, sel, parts):
    out = None
    for _ in range(parts):
        piece = x.astype(BF16)
        term = jnp.dot(piece, sel, preferred_element_type=F32)
        out = term if out is None else out + term
        x = x - piece.astype(F32)
    return out


def _head_sum(x, sel_ref, selt_ref):
    return _split_dot(_split_dot(x, sel_ref[...], 2), selt_ref[...], 2)


def _prep_kernel(*refs, tm, tiles_per_batch, first):
    if first:
        (big_ref, sm_ref, mub_ref, mus_ref, wup_ref, w0_ref, aup_ref, a0_ref, gup_ref,
         kk_ref, ka_ref, rk_ref, sel_ref, selt_ref,
         r_o, lw_o, k_o, v_o, kn_o, a_o, g_o, bg_o, vf_o, cb_sc, cs_sc) = refs
    else:
        (big_ref, sm_ref, mub_ref, mus_ref, wup_ref, w0_ref, aup_ref, a0_ref, gup_ref,
         vup_ref, v0_ref, vf_ref, kk_ref, ka_ref, rk_ref, sel_ref, selt_ref,
         r_o, lw_o, k_o, v_o, kn_o, a_o, g_o, bg_o, cb_sc, cs_sc) = refs
    i = pl.program_id(0)

    @pl.when(i == 0)
    def _():
        cb_sc[...] = jnp.zeros_like(cb_sc)
        cs_sc[...] = jnp.zeros_like(cs_sc)

    ri = lax.broadcasted_iota(jnp.int32, (tm, 1), 0)
    real = (lax.rem(i, tiles_per_batch) * tm + ri) >= FRONT_PAD

    def mix(x_ref, mu_ref, carry):
        h = jnp.where(real, x_ref[...], 0.0)
        prev = jnp.where(ri == 0, carry[...], pltpu.roll(h, 1, axis=0))
        carry[...] = h[tm - 1:tm, :]
        p = h + mu_ref[...] * (prev - h)
        return jnp.where(real, p, 0.0)

    pb = mix(big_ref, mub_ref, cb_sc)
    ps = mix(sm_ref, mus_ref, cs_sc)
    r = pb[:, 0:WIDTH]
    k = pb[:, WIDTH:2 * WIDTH]
    v = pb[:, 2 * WIDTH:3 * WIDTH]

    def lora(x, w_ref):
        return jnp.dot(x.astype(BF16), w_ref[...], preferred_element_type=F32)

    lw = -DECAY_SCALE * jax.nn.sigmoid(w0_ref[...] + lora(jnp.tanh(ps[:, SM_W:SM_A]), wup_ref))
    a = jax.nn.sigmoid(a0_ref[...] + lora(ps[:, SM_A:SM_G], aup_ref))
    g = lora(jax.nn.sigmoid(ps[:, SM_G:SM_V]), gup_ref)
    if first:
        vf_o[...] = v
    else:
        v = v + (vf_ref[...] - v) * jax.nn.sigmoid(v0_ref[...] + lora(ps[:, SM_V:SM_COLS], vup_ref))

    kn = k * kk_ref[...]
    ss = _split_dot(kn * kn, sel_ref[...], 2)
    inv = 1.0 / jnp.maximum(jnp.sqrt(ss), 1e-12)
    kn = kn * _split_dot(inv, selt_ref[...], 2)
    k = k * (1.0 + (a - 1.0) * ka_ref[...])
    bonus = _head_sum(r * k * rk_ref[...], sel_ref, selt_ref) * v

    lw_o[...] = lw
    r_o[...] = r.astype(BF16)
    k_o[...] = k.astype(BF16)
    v_o[...] = v.astype(BF16)
    kn_o[...] = kn.astype(BF16)
    a_o[...] = a.astype(BF16)
    g_o[...] = g.astype(BF16)
    bg_o[...] = (bonus * g).astype(BF16)


def _rwkv_prep(big, small, lp, sel, selt, v_first, tp):
    n = big.shape[0]
    first = v_first is None
    tm = _tile(tp, 320)
    row = lambda c: pl.BlockSpec((tm, c), lambda i: (i, 0))
    full = lambda a: pl.BlockSpec(a.shape, lambda i: (0,) * a.ndim)
    ins = [big, small, lp["mu_big"], lp["mu_small"], lp["w_up"], lp["w0"], lp["a_up"], lp["a0"], lp["g_up"]]
    specs = [row(3 * WIDTH), row(SM_COLS)] + [full(a) for a in ins[2:]]
    if not first:
        ins += [lp["v_up"], lp["v0"], v_first]
        specs += [full(lp["v_up"]), full(lp["v0"]), row(WIDTH)]
    tail = [lp["k_k"], lp["k_a"], lp["r_k"], sel, selt]
    ins += tail
    specs += [full(a) for a in tail]
    dtypes = [BF16, F32] + [BF16] * 6 + ([F32] if first else [])
    kern = functools.partial(_prep_kernel, tm=tm, tiles_per_batch=tp // tm, first=first)
    return pl.pallas_call(
        kern,
        grid=(n // tm,),
        in_specs=specs,
        out_specs=[row(WIDTH)] * len(dtypes),
        out_shape=[jax.ShapeDtypeStruct((n, WIDTH), dt) for dt in dtypes],
        scratch_shapes=[pltpu.VMEM((1, 3 * WIDTH), F32), pltpu.VMEM((1, SM_COLS), F32)],
        compiler_params=_params(("arbitrary",)),
        name="rwkv_prep",
    )(*ins)


def _scan_kernel(r_ref, lw_ref, k_ref, v_ref, kn_ref, a_ref, g_ref, bg_ref, lg_ref, lb_ref,
                 o_ref, s_sc, *, tt):
    ti = pl.program_id(1)

    @pl.when(ti == 0)
    def _():
        s_sc[...] = jnp.zeros_like(s_sc)

    w = SCAN_W
    ri = lax.broadcasted_iota(jnp.int32, (w, w), 0)
    ci = lax.broadcasted_iota(jnp.int32, (w, w), 1)
    same_head = (ri // HEAD_DIM) == (ci // HEAD_DIM)
    strict = same_head & (ci < ri)
    incl = same_head & (ci <= ri)
    eye = (ri == ci).astype(F32)
    tri_r = lax.broadcasted_iota(jnp.int32, (CHUNK, CHUNK), 0)
    tri_c = lax.broadcasted_iota(jnp.int32, (CHUNK, CHUNK), 1)
    tri = (tri_c <= tri_r).astype(BF16)

    def stack(x):
        return jnp.where(same_head, jnp.tile(x, (SCAN_HEADS, 1)), 0.0).astype(BF16)

    def mm(x, y):
        return jnp.dot(x, y, preferred_element_type=F32)

    def mm_nt(x, y):
        return lax.dot_general(x, y, (((1,), (1,)), ((), ())), preferred_element_type=F32)

    def mm_tn(x, y):
        return lax.dot_general(x, y, (((0,), (0,)), ((), ())), preferred_element_type=F32)

    groups = range(WIDTH // w)

    def prepare(rows, gi):
        cols = slice(gi * w, (gi + 1) * w)
        lw = lw_ref[rows, cols]
        r, k, v, kn, a = (x[rows, cols].astype(F32) for x in (r_ref, k_ref, v_ref, kn_ref, a_ref))
        cum = None
        rest = lw
        for _ in range(3):
            piece = rest.astype(BF16)
            term = jnp.dot(tri, piece, preferred_element_type=F32)
            cum = term if cum is None else cum + term
            rest = rest - piece.astype(F32)
        p_in = jnp.exp(cum)
        p_ex = jnp.exp(cum - lw)
        p_inv = jnp.exp(-cum)
        p_end = p_in[CHUNK - 1:CHUNK, :]
        bt = kn * a * p_inv
        kt = k * p_inv
        ar = jnp.concatenate([stack(-kn * p_ex), stack(r * p_in)], axis=0)
        bk = jnp.concatenate([stack(bt), stack(kt)], axis=0)
        return dict(ar=ar, bk=bk, vs=stack(v), p_end=p_end,
                    bh=stack(bt * p_end), kh=stack(kt * p_end))

    def chunk(c, carry):
        rows = pl.ds(pl.multiple_of(c * CHUNK, CHUNK), CHUNK)
        g = [prepare(rows, gi) for gi in groups]
        gram = [mm_nt(x["ar"], x["bk"]) for x in g]
        a_ab = [jnp.where(strict, m[0:w, 0:w], 0.0) for m in gram]
        a_kv = [jnp.concatenate([jnp.where(strict, m[0:w, w:2 * w], 0.0),
                                 jnp.where(incl, m[w:2 * w, w:2 * w], 0.0)], axis=0).astype(BF16)
                for m in gram]
        a_rb = [jnp.where(incl, m[w:2 * w, 0:w], 0.0).astype(BF16) for m in gram]
        s0 = [s_sc[gi] for gi in groups]
        xr = [mm_nt(x["ar"], s.astype(BF16)) for x, s in zip(g, s0)]
        av = [mm(m, x["vs"]) for m, x in zip(a_kv, g)]
        inv = [eye + m for m in a_ab]
        apow = [m.astype(BF16) for m in a_ab]
        for _ in range(int(math.log2(CHUNK)) - 1):
            apow = [mm(m, m).astype(BF16) for m in apow]
            inv = [t + mm(t.astype(BF16), m) for t, m in zip(inv, apow)]
        ub = [mm(t.astype(BF16), (x_[0:w] + v_[0:w]).astype(BF16)).astype(BF16)
              for t, x_, v_ in zip(inv, xr, av)]
        y4 = [x_[w:2 * w] + mm(m, u_) + v_[w:2 * w] for x_, m, u_, v_ in zip(xr, a_rb, ub, av)]
        for gi in groups:
            s_sc[gi] = (s0[gi] * g[gi]["p_end"] + mm_tn(ub[gi], g[gi]["bh"])
                        + mm_tn(g[gi]["vs"], g[gi]["kh"]))
        for gi in groups:
            cols = slice(gi * w, (gi + 1) * w)
            mean = jnp.sum(y4[gi], axis=1, keepdims=True) * (1.0 / HEAD_DIM)
            yc = jnp.where(same_head, y4[gi] - mean, 0.0)
            var = jnp.sum(yc * yc, axis=1, keepdims=True) * (1.0 / HEAD_DIM)
            yn4 = yc * lax.rsqrt(var + GN_EPS)
            yn = yn4[0:CHUNK]
            for h in range(1, SCAN_HEADS):
                yn = yn + yn4[h * CHUNK:(h + 1) * CHUNK]
            yn = yn * lg_ref[:, cols] + lb_ref[:, cols]
            o_ref[rows, cols] = (yn * g_ref[rows, cols].astype(F32)
                                 + bg_ref[rows, cols].astype(F32)).astype(o_ref.dtype)
        return carry

    lax.fori_loop(0, tt // CHUNK, chunk, 0)


def _rwkv_scan(r, lw, k, v, kn, a, g, bg, lnx_g, lnx_b, b, tp):
    n = r.shape[0]
    tt = _tile(tp, 320)
    nt = tp // tt
    spec = pl.BlockSpec((tt, WIDTH), lambda bi, ti: (bi * nt + ti, 0))
    vec = pl.BlockSpec((1, WIDTH), lambda bi, ti: (0, 0))
    kern = functools.partial(_scan_kernel, tt=tt)
    return pl.pallas_call(
        kern,
        grid=(b, nt),
        in_specs=[spec] * 8 + [vec, vec],
        out_specs=spec,
        out_shape=jax.ShapeDtypeStruct((n, WIDTH), BF16),
        scratch_shapes=[pltpu.VMEM((WIDTH // SCAN_W, SCAN_W, SCAN_W), F32)],
        compiler_params=_params(("parallel", "arbitrary")),
        name="rwkv_scan",
    )(r, lw, k, v, kn, a, g, bg, lnx_g, lnx_b)


def _outproj_kernel(fox_ref, rwkv_ref, h_ref, wt_ref, wb_ref, g_ref, b_ref, of_ref, ob_ref):
    mixed = (jnp.dot(fox_ref[...], wt_ref[...], preferred_element_type=F32)
             + jnp.dot(rwkv_ref[...], wb_ref[...], preferred_element_type=F32))
    y = _layer_norm(ALPHA * h_ref[...] + mixed, g_ref[...], b_ref[...])
    of_ref[...] = y
    ob_ref[...] = y.astype(BF16)


def _outproj_ln(fox, rwkv, h, w_out, g, b, tp):
    n, d = h.shape
    tm = _tile(tp, 320)
    half = pl.BlockSpec((tm, WIDTH), lambda i: (i, 0))
    row = pl.BlockSpec((tm, d), lambda i: (i, 0))
    vec = pl.BlockSpec((1, d), lambda i: (0, 0))
    return pl.pallas_call(
        _outproj_kernel,
        grid=(n // tm,),
        in_specs=[half, half, row,
                  pl.BlockSpec((WIDTH, d), lambda i: (0, 0)),
                  pl.BlockSpec((WIDTH, d), lambda i: (1, 0)),
                  vec, vec],
        out_specs=[row, row],
        out_shape=[jax.ShapeDtypeStruct((n, d), F32), jax.ShapeDtypeStruct((n, d), BF16)],
        compiler_params=_params(("parallel",)),
        name="outproj_ln",
    )(fox, rwkv, h, w_out, w_out, g, b)


def _ffn_kernel(xb_ref, xf_ref, w1_ref, w2_ref, g_ref, b_ref, of_ref, ob_ref, acc_sc):
    j = pl.program_id(1)

    @pl.when(j == 0)
    def _():
        acc_sc[...] = jnp.zeros_like(acc_sc)

    u = jnp.maximum(jnp.dot(xb_ref[...], w1_ref[...], preferred_element_type=F32), 0.0)
    acc_sc[...] += jnp.dot((u * u).astype(BF16), w2_ref[...], preferred_element_type=F32)

    @pl.when(j == pl.num_programs(1) - 1)
    def _():
        y = _layer_norm(ALPHA * xf_ref[...] + acc_sc[...], g_ref[...], b_ref[...])
        of_ref[...] = y
        ob_ref[...] = y.astype(BF16)


def _ffn_ln(xb, xf, w1, w2, g, b, tp):
    n, d = xf.shape
    tm = _tile(tp, 640)
    tf = 512
    row = pl.BlockSpec((tm, d), lambda i, j: (i, 0))
    vec = pl.BlockSpec((1, d), lambda i, j: (0, 0))
    return pl.pallas_call(
        _ffn_kernel,
        grid=(n // tm, D_FF // tf),
        in_specs=[row, row,
                  pl.BlockSpec((d, tf), lambda i, j: (0, j)),
                  pl.BlockSpec((tf, d), lambda i, j: (j, 0)),
                  vec, vec],
        out_specs=[row, row],
        out_shape=[jax.ShapeDtypeStruct((n, d), F32), jax.ShapeDtypeStruct((n, d), BF16)],
        scratch_shapes=[pltpu.VMEM((tm, d), F32)],
        compiler_params=_params(("parallel", "arbitrary")),
        name="ffn_ln",
    )(xb, xf, w1, w2, g, b)


def _pad_cols(a, width):
    return jnp.pad(a, ((0, 0), (0, width - a.shape[1])))


def _pad_rows(a, height):
    return jnp.pad(a, ((0, height - a.shape[0]), (0, 0)))


def _layer_params(l, w_in, mu, fox_fb, fox_out_g, w_up, w0, a_up, a0, g_up, v_up, v0,
                  k_k, k_a, r_k, lnx_g, lnx_b, w_out, ln1_g, ln1_b, w_ff1, w_ff2, ln2_g, ln2_b):
    first = v_up is None
    fq, fk, fv, ff = 0, WIDTH, 2 * WIDTH, 3 * WIDTH
    rs = 3 * WIDTH + HEADS
    o_w = 3 * WIDTH
    o_a = o_w + DECAY_RANK
    o_g = o_a + AAA_RANK
    o_v = o_g + GATE_RANK
    col_scale = jnp.where(jnp.arange(ff) < fk, HEAD_DIM ** -0.5 * LOG2E, 1.0).astype(F32)
    w_fox = (w_in[:, fq:ff] * col_scale).astype(BF16)
    wr = w_in[:, rs:]
    w_big = w_in[:, rs:rs + o_w].astype(BF16)
    smalls = [(w_in[:, ff:rs], SM_W - SM_F), (wr[:, o_w:o_a], SM_A - SM_W), (wr[:, o_a:o_g], SM_G - SM_A),
              (wr[:, o_g:o_v], SM_V - SM_G)]
    mus = [(jnp.zeros((1, HEADS), F32), SM_W - SM_F), (mu[None, o_w:o_a], SM_A - SM_W),
           (mu[None, o_a:o_g], SM_G - SM_A), (mu[None, o_g:o_v], SM_V - SM_G)]
    if first:
        smalls.append((jnp.zeros((D_MODEL, 0), F32), SM_COLS - SM_V))
        mus.append((jnp.zeros((1, 0), F32), SM_COLS - SM_V))
    else:
        smalls.append((wr[:, o_v:], SM_COLS - SM_V))
        mus.append((mu[None, o_v:], SM_COLS - SM_V))
    w_small = jnp.concatenate([_pad_cols(a, wd) for a, wd in smalls], axis=1).astype(BF16)
    mu_small = jnp.concatenate([_pad_cols(a, wd) for a, wd in mus], axis=1)
    row = lambda a: a.reshape(1, -1)
    lp = dict(
        w_fox=w_fox, w_big=w_big, w_small=w_small,
        mu_big=mu[None, 0:o_w], mu_small=mu_small,
        fb=_pad_cols(fox_fb[None, :], LANES),
        gains=fox_out_g.reshape(HEADS // 2, 1, 2 * HEAD_DIM),
        w_up=_pad_rows(w_up, SM_A - SM_W).astype(BF16), w0=row(w0),
        a_up=_pad_rows(a_up, SM_G - SM_A).astype(BF16), a0=row(a0),
        g_up=_pad_rows(g_up, SM_V - SM_G).astype(BF16),
        k_k=row(k_k), k_a=row(k_a), r_k=row(r_k), lnx_g=row(lnx_g), lnx_b=row(lnx_b),
        w_out=w_out.astype(BF16),
        ln1_g=row(ln1_g), ln1_b=row(ln1_b),
        w_ff1=w_ff1.astype(BF16), w_ff2=w_ff2.astype(BF16),
        ln2_g=row(ln2_g), ln2_b=row(ln2_b),
    )
    if not first:
        lp["v_up"] = _pad_rows(v_up, SM_COLS - SM_V).astype(BF16)
        lp["v0"] = row(v0)
    return lp


def _layer(hf, hb, lp, sel, selt, v_first, b, tp):
    qkv = _matmul(hb, lp["w_fox"], BF16, "inproj_fox")
    big = _matmul(hb, lp["w_big"], F32, "inproj_rwkv")
    small = _matmul(hb, lp["w_small"], F32, "inproj_small")
    c = _fox_cumsum(small, lp["fb"], b, tp)
    qp, kp, vt = _fox_prep(qkv, c, b, tp)
    fox = _fox_attention(qp, kp, vt, lp["gains"], b, tp)
    outs = _rwkv_prep(big, small, lp, sel, selt, v_first, tp)
    r, lw, k, v, kn, a, g, bg = outs[:8]
    if v_first is None:
        v_first = outs[8]
    rwkv = _rwkv_scan(r, lw, k, v, kn, a, g, bg, lp["lnx_g"], lp["lnx_b"], b, tp)
    xf, xb = _outproj_ln(fox, rwkv, hf, lp["w_out"], lp["ln1_g"], lp["ln1_b"], tp)
    hf, hb = _ffn_ln(xb, xf, lp["w_ff1"], lp["w_ff2"], lp["ln2_g"], lp["ln2_b"], tp)
    return hf, hb, v_first


def kernel(x, meta, ln_in_g, ln_in_b, w_in_first, w_in_rest, mu_first, mu_rest, fox_fb, fox_out_g, w_up, w0, a_up, a0, g_up, v_up, v0, k_k, k_a, r_k, lnx_g, lnx_b, w_out, ln1_g, ln1_b, w_ff1, w_ff2, ln2_g, ln2_b):
    b, seq, d = x.shape
    assert d == D_MODEL
    tp = FRONT_PAD + N_META + seq
    assert tp % LANES == 0
    head = jnp.concatenate([jnp.zeros((FRONT_PAD, d), x.dtype), meta.astype(x.dtype)], axis=0)
    h0 = jnp.concatenate([jnp.broadcast_to(head[None], (b, LANES, d)), x], axis=1).reshape(b * tp, d)
    hf, hb = _ln_rows(h0, ln_in_g, ln_in_b, tp)

    lane_head = jnp.arange(WIDTH, dtype=jnp.int32) // HEAD_DIM
    sel = (lane_head[:, None] == jnp.arange(LANES, dtype=jnp.int32)[None, :]).astype(BF16)
    selt = sel.T

    v_first = None
    for l in range(DEPTH):
        first = l == 0
        lp = _layer_params(
            l, w_in_first if first else w_in_rest[l - 1], mu_first if first else mu_rest[l - 1],
            fox_fb[l], fox_out_g[l], w_up[l], w0[l], a_up[l], a0[l], g_up[l],
            None if first else v_up[l - 1], None if first else v0[l - 1],
            k_k[l], k_a[l], r_k[l], lnx_g[l], lnx_b[l], w_out[l],
            ln1_g[l], ln1_b[l], w_ff1[l], w_ff2[l], ln2_g[l], ln2_b[l])
        hf, hb, v_first = _layer(hf, hb, lp, sel, selt, v_first, b, tp)
    return hf.reshape(b, tp, d)[:, LANES:, :]
```

```python
import functools
import math

import jax
import jax.numpy as jnp
from jax import lax
from jax.experimental import pallas as pl
from jax.experimental.pallas import tpu as pltpu

F32 = jnp.float32
BF16 = jnp.bfloat16
HIGHEST = lax.Precision.HIGHEST

D_MODEL = 2048
DEPTH = 4
N_META = 16
LANES = 128
FRONT_PAD = (-N_META) % LANES
HEAD_DIM = 64
WIDTH = D_MODEL // 2
HEADS = WIDTH // HEAD_DIM
DECAY_RANK = 64
AAA_RANK = 64
GATE_RANK = 160
MV_RANK = 32
D_FF = 4 * D_MODEL
ALPHA = (2 * DEPTH) ** 0.25
LN_EPS = 1e-5
GN_EPS = 64e-5
OUT_NORM_EPS = 1e-6
DECAY_SCALE = math.exp(-0.5)
NEG_INF = -1e30
MASKED_KEY = 1e30
LOG2E = math.log2(math.e)
C_PARTS = 3
DENOM_LANE = (HEAD_DIM, HEAD_DIM - 1)
VT_ROWS = ((0, HEAD_DIM + 16), (HEAD_DIM - 16, LANES))

SM_F, SM_W, SM_A, SM_G, SM_V, SM_COLS = 0, 128, 256, 384, 640, 768

CHUNK = 64
SCAN_HEADS = 4
SCAN_W = SCAN_HEADS * HEAD_DIM
VMEM_LIMIT = 56 * 1024 * 1024


def _tile(total, cap, mult=64):
    best = None
    for d in range(mult, cap + 1, mult):
        if total % d == 0:
            best = d
    assert best is not None, (total, cap)
    return best


def _params(sem):
    return pltpu.CompilerParams(dimension_semantics=sem, vmem_limit_bytes=VMEM_LIMIT)


def _layer_norm(x, g, b):
    mean = jnp.mean(x, -1, keepdims=True)
    xc = x - mean
    var = jnp.mean(xc * xc, -1, keepdims=True)
    return xc * lax.rsqrt(var + LN_EPS) * g + b


def _ln_kernel(x_ref, g_ref, b_ref, of_ref, ob_ref):
    y = _layer_norm(x_ref[...], g_ref[...], b_ref[...])
    of_ref[...] = y
    ob_ref[...] = y.astype(BF16)


def _ln_rows(x, g, b, tp):
    n, d = x.shape
    tm = _tile(tp, 320)
    row = pl.BlockSpec((tm, d), lambda i: (i, 0))
    vec = pl.BlockSpec((1, d), lambda i: (0, 0))
    return pl.pallas_call(
        _ln_kernel,
        grid=(n // tm,),
        in_specs=[row, vec, vec],
        out_specs=[row, row],
        out_shape=[jax.ShapeDtypeStruct((n, d), F32), jax.ShapeDtypeStruct((n, d), BF16)],
        compiler_params=_params(("parallel",)),
        name="ln_in",
    )(x, g.reshape(1, d), b.reshape(1, d))


def _mm_kernel(x_ref, w_ref, o_ref):
    o_ref[...] = jnp.dot(x_ref[...], w_ref[...], preferred_element_type=F32).astype(o_ref.dtype)


def _matmul(x, w, out_dtype, name):
    n, k = x.shape
    nc = w.shape[1]
    tm = _tile(n, 1664)
    tn = _tile(nc, 768, LANES)
    return pl.pallas_call(
        _mm_kernel,
        grid=(n // tm, nc // tn),
        in_specs=[pl.BlockSpec((tm, k), lambda i, j: (i, 0)),
                  pl.BlockSpec((k, tn), lambda i, j: (0, j))],
        out_specs=pl.BlockSpec((tm, tn), lambda i, j: (i, j)),
        out_shape=jax.ShapeDtypeStruct((n, nc), out_dtype),
        compiler_params=_params(("parallel", "parallel")),
        name=name,
    )(x, w)


def _fox_prep_kernel(qkv_ref, f_ref, fb_ref, qo_ref, ko_ref, vo_ref, carry_ref, *, tm):
    t = pl.program_id(1)

    @pl.when(t == 0)
    def _():
        carry_ref[...] = jnp.zeros_like(carry_ref)

    z = f_ref[...] + fb_ref[...]
    lf = jnp.minimum(z, 0.0) - jnp.log1p(jnp.exp(-jnp.abs(z)))
    real = (t * tm + lax.broadcasted_iota(jnp.int32, (tm, LANES), 0)) >= FRONT_PAD
    lf = jnp.where(real, lf, 0.0)
    ri = lax.broadcasted_iota(jnp.int32, (tm, tm), 0)
    ci = lax.broadcasted_iota(jnp.int32, (tm, tm), 1)
    tri = (ci <= ri).astype(F32)
    c = jnp.dot(tri, lf, precision=HIGHEST, preferred_element_type=F32) + carry_ref[...]
    carry_ref[...] = c[tm - 1:tm, :]
    c = jnp.where(real, c, MASKED_KEY)

    lane = lax.broadcasted_iota(jnp.int32, (tm, LANES), 1)
    low = lane < HEAD_DIM
    cs = c * LOG2E
    hi = cs.astype(BF16).astype(F32)
    rest = cs - hi
    mid = rest.astype(BF16).astype(F32)
    lo = rest - mid
    for j in range(HEADS // 2):
        qb = qkv_ref[:, j * LANES:(j + 1) * LANES].astype(F32)
        kb = qkv_ref[:, WIDTH + j * LANES:WIDTH + (j + 1) * LANES].astype(F32)
        vb = qkv_ref[:, 2 * WIDTH + j * LANES:2 * WIDTH + (j + 1) * LANES].astype(F32)
        for hh in range(2):
            h = 2 * j + hh
            data = low if hh == 0 else jnp.logical_not(low)
            x0 = HEAD_DIM if hh == 0 else 0
            ones = jnp.where((lane >= x0) & (lane < x0 + C_PARTS), 1.0, 0.0)
            cpart = jnp.where(lane == x0, -hi[:, h:h + 1],
                              jnp.where(lane == x0 + 1, -mid[:, h:h + 1],
                                        jnp.where(lane == x0 + 2, -lo[:, h:h + 1], 0.0)))
            qo_ref[0, h] = jnp.where(data, qb, ones).astype(BF16)
            ko_ref[0, h] = jnp.where(data, kb, cpart).astype(BF16)
            vx = jnp.where(data, vb, jnp.where(lane == DENOM_LANE[hh], 1.0, 0.0))
            vo_ref[0, h] = vx.T.astype(BF16)


def _fox_prep(qkv, proj, fb_pad, b, tp):
    tm = _tile(tp, 640, LANES)
    nt = tp // tm
    f_block = (3 * WIDTH + SM_F) // LANES
    out = pl.BlockSpec((1, HEADS, tm, LANES), lambda bi, t: (bi, 0, t, 0))
    out_t = pl.BlockSpec((1, HEADS, LANES, tm), lambda bi, t: (bi, 0, 0, t))
    shape = jax.ShapeDtypeStruct((b, HEADS, tp, LANES), BF16)
    shape_t = jax.ShapeDtypeStruct((b, HEADS, LANES, tp), BF16)
    return pl.pallas_call(
        functools.partial(_fox_prep_kernel, tm=tm),
        grid=(b, nt),
        in_specs=[pl.BlockSpec((tm, 3 * WIDTH), lambda bi, t: (bi * nt + t, 0)),
                  pl.BlockSpec((tm, LANES), lambda bi, t: (bi * nt + t, f_block)),
                  pl.BlockSpec((1, LANES), lambda bi, t: (0, 0))],
        out_specs=[out, out, out_t],
        out_shape=[shape, shape, shape_t],
        scratch_shapes=[pltpu.VMEM((1, LANES), F32)],
        compiler_params=_params(("parallel", "arbitrary")),
        name="fox_prep",
    )(qkv, proj, fb_pad)


def _fox_kernel(q_ref, k_ref, v_ref, g_ref, o_ref, m_sc, acc_sc, sa_sc, sb_sc, *, blk):
    qi = pl.program_id(2)
    ri = lax.broadcasted_iota(jnp.int32, (blk, blk), 0)
    ci = lax.broadcasted_iota(jnp.int32, (blk, blk), 1)
    causal = ri <= ci
    for h in range(2):
        m_sc[h] = jnp.full((1, blk), NEG_INF, F32)
        acc_sc[h] = jnp.zeros((LANES, blk), F32)

    def key_rows(j):
        return pl.ds(pl.multiple_of(j * blk, LANES), blk)

    def scores(j, s_sc):
        for h in range(2):
            s_sc[h] = lax.dot_general(k_ref[0, h, key_rows(j), :], q_ref[0, h],
                                      (((1,), (1,)), ((), ())), preferred_element_type=F32)

    def consume(j, s_sc, masked):
        for h in range(2):
            s = s_sc[h]
            if masked:
                s = jnp.where(causal, s, NEG_INF)
            lo, hi = VT_ROWS[h]
            m_prev = m_sc[h]
            m_next = jnp.maximum(m_prev, jnp.max(s, axis=0, keepdims=True))
            p = jnp.exp2(s - m_next)
            acc_sc[h, lo:hi, :] = (jnp.exp2(m_prev - m_next) * acc_sc[h, lo:hi, :]
                                   + jnp.dot(v_ref[0, h, lo:hi, key_rows(j)], p.astype(BF16),
                                             preferred_element_type=F32))
            m_sc[h] = m_next

    scores(0, sa_sc)

    def pair(p, carry):
        j = 2 * p
        scores(j + 1, sb_sc)
        consume(j, sa_sc, False)
        scores(j + 2, sa_sc)
        consume(j + 1, sb_sc, False)
        return carry

    lax.fori_loop(0, qi // 2, pair, 0)

    @pl.when(qi % 2 == 1)
    def _():
        scores(qi, sb_sc)
        consume(qi - 1, sa_sc, False)
        consume(qi, sb_sc, True)

    @pl.when(qi % 2 == 0)
    def _():
        consume(qi, sa_sc, True)

    lane = lax.broadcasted_iota(jnp.int32, (blk, LANES), 1)
    low = lane < HEAD_DIM
    outs = []
    for h in range(2):
        data = low if h == 0 else jnp.logical_not(low)
        acc = acc_sc[h].T
        o = jnp.where(data, acc / acc[:, DENOM_LANE[h]:DENOM_LANE[h] + 1], 0.0)
        ms = jnp.sum(o * o, axis=1, keepdims=True) * (1.0 / HEAD_DIM)
        outs.append(o * lax.rsqrt(ms + OUT_NORM_EPS))
    o_ref[...] = ((outs[0] + outs[1]) * g_ref[0]).astype(o_ref.dtype)


def _fox_attention(qp, kp, vt, gains, b, tp):
    blk = _tile(tp, 640, LANES)
    nq = tp // blk
    hp = HEADS // 2
    kern = functools.partial(_fox_kernel, blk=blk)
    return pl.pallas_call(
        kern,
        grid=(b, hp, nq),
        in_specs=[pl.BlockSpec((1, 2, blk, LANES), lambda bi, h, qi: (bi, h, qi, 0)),
                  pl.BlockSpec((1, 2, tp, LANES), lambda bi, h, qi: (bi, h, 0, 0)),
                  pl.BlockSpec((1, 2, LANES, tp), lambda bi, h, qi: (bi, h, 0, 0)),
                  pl.BlockSpec((1, 1, LANES), lambda bi, h, qi: (h, 0, 0))],
        out_specs=pl.BlockSpec((blk, LANES), lambda bi, h, qi: (bi * nq + qi, h)),
        out_shape=jax.ShapeDtypeStruct((b * tp, WIDTH), BF16),
        scratch_shapes=[pltpu.VMEM((2, 1, blk), F32), pltpu.VMEM((2, LANES, blk), F32),
                        pltpu.VMEM((2, blk, blk), F32), pltpu.VMEM((2, blk, blk), F32)],
        compiler_params=_params(("parallel", "parallel", "arbitrary")),
        name="fox_attention",
    )(qp, kp, vt, gains)


def _split_dot(x, sel, parts):
    out = None
    for _ in range(parts):
        piece = x.astype(BF16)
        term = jnp.dot(piece, sel, preferred_element_type=F32)
        out = term if out is None else out + term
        x = x - piece.astype(F32)
    return out


def _head_sum(x, sel_ref, selt_ref):
    return _split_dot(_split_dot(x, sel_ref[...], 2), selt_ref[...], 2)


def _prep_kernel(*refs, tm, tiles_per_batch, first):
    if first:
        (big_ref, sm_ref, mub_ref, mus_ref, wup_ref, w0_ref, aup_ref, a0_ref, gup_ref,
         kk_ref, ka_ref, rk_ref, sel_ref, selt_ref,
         r_o, lw_o, k_o, v_o, kn_o, a_o, g_o, bg_o, vf_o, cb_sc, cs_sc) = refs
    else:
        (big_ref, sm_ref, mub_ref, mus_ref, wup_ref, w0_ref, aup_ref, a0_ref, gup_ref,
         vup_ref, v0_ref, vf_ref, kk_ref, ka_ref, rk_ref, sel_ref, selt_ref,
         r_o, lw_o, k_o, v_o, kn_o, a_o, g_o, bg_o, cb_sc, cs_sc) = refs
    i = pl.program_id(0)

    @pl.when(i == 0)
    def _():
        cb_sc[...] = jnp.zeros_like(cb_sc)
        cs_sc[...] = jnp.zeros_like(cs_sc)

    ri = lax.broadcasted_iota(jnp.int32, (tm, 1), 0)
    real = (lax.rem(i, tiles_per_batch) * tm + ri) >= FRONT_PAD

    def mix(x_ref, mu_ref, carry):
        h = jnp.where(real, x_ref[...], 0.0)
        prev = jnp.where(ri == 0, carry[...], pltpu.roll(h, 1, axis=0))
        carry[...] = h[tm - 1:tm, :]
        p = h + mu_ref[...] * (prev - h)
        return jnp.where(real, p, 0.0)

    pb = mix(big_ref, mub_ref, cb_sc)
    ps = mix(sm_ref, mus_ref, cs_sc)
    r = pb[:, 0:WIDTH]
    k = pb[:, WIDTH:2 * WIDTH]
    v = pb[:, 2 * WIDTH:3 * WIDTH]

    def lora(x, w_ref):
        return jnp.dot(x.astype(BF16), w_ref[...], preferred_element_type=F32)

    lw = -DECAY_SCALE * jax.nn.sigmoid(w0_ref[...] + lora(jnp.tanh(ps[:, SM_W:SM_A]), wup_ref))
    a = jax.nn.sigmoid(a0_ref[...] + lora(ps[:, SM_A:SM_G], aup_ref))
    g = lora(jax.nn.sigmoid(ps[:, SM_G:SM_V]), gup_ref)
    if first:
        vf_o[...] = v
    else:
        v = v + (vf_ref[...] - v) * jax.nn.sigmoid(v0_ref[...] + lora(ps[:, SM_V:SM_COLS], vup_ref))

    kn = k * kk_ref[...]
    ss = _split_dot(kn * kn, sel_ref[...], 2)
    inv = 1.0 / jnp.maximum(jnp.sqrt(ss), 1e-12)
    kn = kn * _split_dot(inv, selt_ref[...], 2)
    k = k * (1.0 + (a - 1.0) * ka_ref[...])
    bonus = _head_sum(r * k * rk_ref[...], sel_ref, selt_ref) * v

    lw_o[...] = lw
    r_o[...] = r.astype(BF16)
    k_o[...] = k.astype(BF16)
    v_o[...] = v.astype(BF16)
    kn_o[...] = kn.astype(BF16)
    a_o[...] = a.astype(BF16)
    g_o[...] = g.astype(BF16)
    bg_o[...] = (bonus * g).astype(BF16)


def _rwkv_prep(proj, lp, sel, selt, v_first, tp):
    n = proj.shape[0]
    first = v_first is None
    tm = _tile(tp, 320)
    row = lambda c: pl.BlockSpec((tm, c), lambda i: (i, 0))
    full = lambda a: pl.BlockSpec(a.shape, lambda i: (0,) * a.ndim)
    small_block = 3 * WIDTH // SM_COLS
    ins = [proj, proj, lp["mu_big"], lp["mu_small"], lp["w_up"], lp["w0"], lp["a_up"], lp["a0"], lp["g_up"]]
    specs = ([row(3 * WIDTH), pl.BlockSpec((tm, SM_COLS), lambda i: (i, small_block))]
             + [full(a) for a in ins[2:]])
    if not first:
        ins += [lp["v_up"], lp["v0"], v_first]
        specs += [full(lp["v_up"]), full(lp["v0"]), row(WIDTH)]
    tail = [lp["k_k"], lp["k_a"], lp["r_k"], sel, selt]
    ins += tail
    specs += [full(a) for a in tail]
    dtypes = [BF16, F32] + [BF16] * 6 + ([F32] if first else [])
    kern = functools.partial(_prep_kernel, tm=tm, tiles_per_batch=tp // tm, first=first)
    return pl.pallas_call(
        kern,
        grid=(n // tm,),
        in_specs=specs,
        out_specs=[row(WIDTH)] * len(dtypes),
        out_shape=[jax.ShapeDtypeStruct((n, WIDTH), dt) for dt in dtypes],
        scratch_shapes=[pltpu.VMEM((1, 3 * WIDTH), F32), pltpu.VMEM((1, SM_COLS), F32)],
        compiler_params=_params(("arbitrary",)),
        name="rwkv_prep",
    )(*ins)


def _scan_kernel(r_ref, lw_ref, k_ref, v_ref, kn_ref, a_ref, g_ref, bg_ref, lg_ref, lb_ref,
                 o_ref, s_sc, *, tt):
    ti = pl.program_id(1)

    @pl.when(ti == 0)
    def _():
        s_sc[...] = jnp.zeros_like(s_sc)

    w = SCAN_W
    ri = lax.broadcasted_iota(jnp.int32, (w, w), 0)
    ci = lax.broadcasted_iota(jnp.int32, (w, w), 1)
    same_head = (ri // HEAD_DIM) == (ci // HEAD_DIM)
    strict = same_head & (ci < ri)
    incl = same_head & (ci <= ri)
    eye = (ri == ci).astype(F32)
    tri_r = lax.broadcasted_iota(jnp.int32, (CHUNK, CHUNK), 0)
    tri_c = lax.broadcasted_iota(jnp.int32, (CHUNK, CHUNK), 1)
    tri = (tri_c <= tri_r).astype(BF16)

    def stack(x):
        return jnp.where(same_head, jnp.tile(x, (SCAN_HEADS, 1)), 0.0).astype(BF16)

    def mm(x, y):
        return jnp.dot(x, y, preferred_element_type=F32)

    def mm_nt(x, y):
        return lax.dot_general(x, y, (((1,), (1,)), ((), ())), preferred_element_type=F32)

    def mm_tn(x, y):
        return lax.dot_general(x, y, (((0,), (0,)), ((), ())), preferred_element_type=F32)

    groups = range(WIDTH // w)

    def prepare(rows, gi):
        cols = slice(gi * w, (gi + 1) * w)
        lw = lw_ref[rows, cols]
        r, k, v, kn, a = (x[rows, cols].astype(F32) for x in (r_ref, k_ref, v_ref, kn_ref, a_ref))
        cum = None
        rest = lw
        for _ in range(3):
            piece = rest.astype(BF16)
            term = jnp.dot(tri, piece, preferred_element_type=F32)
            cum = term if cum is None else cum + term
            rest = rest - piece.astype(F32)
        p_in = jnp.exp(cum)
        p_ex = jnp.exp(cum - lw)
        p_inv = jnp.exp(-cum)
        p_end = p_in[CHUNK - 1:CHUNK, :]
        bt = kn * a * p_inv
        kt = k * p_inv
        ar = jnp.concatenate([stack(-kn * p_ex), stack(r * p_in)], axis=0)
        bk = jnp.concatenate([stack(bt), stack(kt)], axis=0)
        return dict(ar=ar, bk=bk, vs=stack(v), p_end=p_end,
                    bh=stack(bt * p_end), kh=stack(kt * p_end))

    def chunk(c, carry):
        rows = pl.ds(pl.multiple_of(c * CHUNK, CHUNK), CHUNK)
        g = [prepare(rows, gi) for gi in groups]
        gram = [mm_nt(x["ar"], x["bk"]) for x in g]
        a_ab = [jnp.where(strict, m[0:w, 0:w], 0.0) for m in gram]
        a_kv = [jnp.concatenate([jnp.where(strict, m[0:w, w:2 * w], 0.0),
                                 jnp.where(incl, m[w:2 * w, w:2 * w], 0.0)], axis=0).astype(BF16)
                for m in gram]
        a_rb = [jnp.where(incl, m[w:2 * w, 0:w], 0.0).astype(BF16) for m in gram]
        s0 = [s_sc[gi] for gi in groups]
        xr = [mm_nt(x["ar"], s.astype(BF16)) for x, s in zip(g, s0)]
        av = [mm(m, x["vs"]) for m, x in zip(a_kv, g)]
        inv = [eye + m for m in a_ab]
        apow = [m.astype(BF16) for m in a_ab]
        for _ in range(int(math.log2(CHUNK)) - 1):
            apow = [mm(m, m).astype(BF16) for m in apow]
            inv = [t + mm(t.astype(BF16), m) for t, m in zip(inv, apow)]
        ub = [mm(t.astype(BF16), (x_[0:w] + v_[0:w]).astype(BF16)).astype(BF16)
              for t, x_, v_ in zip(inv, xr, av)]
        y4 = [x_[w:2 * w] + mm(m, u_) + v_[w:2 * w] for x_, m, u_, v_ in zip(xr, a_rb, ub, av)]
        for gi in groups:
            s_sc[gi] = (s0[gi] * g[gi]["p_end"] + mm_tn(ub[gi], g[gi]["bh"])
                        + mm_tn(g[gi]["vs"], g[gi]["kh"]))
        for gi in groups:
            cols = slice(gi * w, (gi + 1) * w)
            mean = jnp.sum(y4[gi], axis=1, keepdims=True) * (1.0 / HEAD_DIM)
            yc = jnp.where(same_head, y4[gi] - mean, 0.0)
            var = jnp.sum(yc * yc, axis=1, keepdims=True) * (1.0 / HEAD_DIM)
            yn4 = yc * lax.rsqrt(var + GN_EPS)
            yn = yn4[0:CHUNK]
            for h in range(1, SCAN_HEADS):
                yn = yn + yn4[h * CHUNK:(h + 1) * CHUNK]
            yn = yn * lg_ref[:, cols] + lb_ref[:, cols]
            o_ref[rows, cols] = (yn * g_ref[rows, cols].astype(F32)
                                 + bg_ref[rows, cols].astype(F32)).astype(o_ref.dtype)
        return carry

    lax.fori_loop(0, tt // CHUNK, chunk, 0)


def _rwkv_scan(r, lw, k, v, kn, a, g, bg, lnx_g, lnx_b, b, tp):
    n = r.shape[0]
    tt = _tile(tp, 320)
    nt = tp // tt
    spec = pl.BlockSpec((tt, WIDTH), lambda bi, ti: (bi * nt + ti, 0))
    vec = pl.BlockSpec((1, WIDTH), lambda bi, ti: (0, 0))
    kern = functools.partial(_scan_kernel, tt=tt)
    return pl.pallas_call(
        kern,
        grid=(b, nt),
        in_specs=[spec] * 8 + [vec, vec],
        out_specs=spec,
        out_shape=jax.ShapeDtypeStruct((n, WIDTH), BF16),
        scratch_shapes=[pltpu.VMEM((WIDTH // SCAN_W, SCAN_W, SCAN_W), F32)],
        compiler_params=_params(("parallel", "arbitrary")),
        name="rwkv_scan",
    )(r, lw, k, v, kn, a, g, bg, lnx_g, lnx_b)


def _outproj_kernel(fox_ref, rwkv_ref, h_ref, wt_ref, wb_ref, g_ref, b_ref, of_ref, ob_ref):
    mixed = (jnp.dot(fox_ref[...], wt_ref[...], preferred_element_type=F32)
             + jnp.dot(rwkv_ref[...], wb_ref[...], preferred_element_type=F32))
    y = _layer_norm(ALPHA * h_ref[...] + mixed, g_ref[...], b_ref[...])
    of_ref[...] = y
    ob_ref[...] = y.astype(BF16)


def _outproj_ln(fox, rwkv, h, w_out, layer, g, b, tp):
    n, d = h.shape
    tm = _tile(tp, 320)
    half = pl.BlockSpec((tm, WIDTH), lambda i: (i, 0))
    row = pl.BlockSpec((tm, d), lambda i: (i, 0))
    vec = pl.BlockSpec((1, d), lambda i: (0, 0))
    return pl.pallas_call(
        _outproj_kernel,
        grid=(n // tm,),
        in_specs=[half, half, row,
                  pl.BlockSpec((None, WIDTH, d), lambda i: (layer, 0, 0)),
                  pl.BlockSpec((None, WIDTH, d), lambda i: (layer, 1, 0)),
                  vec, vec],
        out_specs=[row, row],
        out_shape=[jax.ShapeDtypeStruct((n, d), F32), jax.ShapeDtypeStruct((n, d), BF16)],
        compiler_params=_params(("parallel",)),
        name="outproj_ln",
    )(fox, rwkv, h, w_out, w_out, g, b)


def _ffn_kernel(xb_ref, xf_ref, w1_ref, w2_ref, g_ref, b_ref, of_ref, ob_ref, acc_sc):
    j = pl.program_id(1)

    @pl.when(j == 0)
    def _():
        acc_sc[...] = jnp.zeros_like(acc_sc)

    u = jnp.maximum(jnp.dot(xb_ref[...], w1_ref[...], preferred_element_type=F32), 0.0)
    acc_sc[...] += jnp.dot((u * u).astype(BF16), w2_ref[...], preferred_element_type=F32)

    @pl.when(j == pl.num_programs(1) - 1)
    def _():
        y = _layer_norm(ALPHA * xf_ref[...] + acc_sc[...], g_ref[...], b_ref[...])
        of_ref[...] = y
        ob_ref[...] = y.astype(BF16)


def _ffn_ln(xb, xf, w1, w2, layer, g, b, tp):
    n, d = xf.shape
    tm = _tile(tp, 640)
    tf = 512
    row = pl.BlockSpec((tm, d), lambda i, j: (i, 0))
    vec = pl.BlockSpec((1, d), lambda i, j: (0, 0))
    return pl.pallas_call(
        _ffn_kernel,
        grid=(n // tm, D_FF // tf),
        in_specs=[row, row,
                  pl.BlockSpec((None, d, tf), lambda i, j: (layer, 0, j)),
                  pl.BlockSpec((None, tf, d), lambda i, j: (layer, j, 0)),
                  vec, vec],
        out_specs=[row, row],
        out_shape=[jax.ShapeDtypeStruct((n, d), F32), jax.ShapeDtypeStruct((n, d), BF16)],
        scratch_shapes=[pltpu.VMEM((tm, d), F32)],
        compiler_params=_params(("parallel", "arbitrary")),
        name="ffn_ln",
    )(xb, xf, w1, w2, g, b)


_RS = 3 * WIDTH + HEADS
_O_W = _RS + 3 * WIDTH
_O_A = _O_W + DECAY_RANK
_O_G = _O_A + AAA_RANK
_O_V = _O_G + GATE_RANK


def _win_layout_kernel(w_ref, wf_ref, wr_ref, *, first):
    w = w_ref[...]
    lane = lax.broadcasted_iota(jnp.int32, (1, 3 * WIDTH), 1)
    col_scale = jnp.where(lane < WIDTH, HEAD_DIM ** -0.5 * LOG2E, 1.0)
    wf_ref[...] = (w[:, 0:3 * WIDTH] * col_scale).astype(BF16)
    s0 = 3 * WIDTH
    wr_ref[:, 0:s0] = w[:, _RS:_O_W].astype(BF16)
    wr_ref[:, s0:s0 + SM_COLS] = jnp.zeros((w.shape[0], SM_COLS), BF16)
    wr_ref[:, s0 + SM_F:s0 + SM_F + HEADS] = w[:, 3 * WIDTH:_RS].astype(BF16)
    wr_ref[:, s0 + SM_W:s0 + SM_W + DECAY_RANK] = w[:, _O_W:_O_A].astype(BF16)
    wr_ref[:, s0 + SM_A:s0 + SM_A + AAA_RANK] = w[:, _O_A:_O_G].astype(BF16)
    wr_ref[:, s0 + SM_G:s0 + SM_G + GATE_RANK] = w[:, _O_G:_O_V].astype(BF16)
    if not first:
        wr_ref[:, s0 + SM_V:s0 + SM_V + MV_RANK] = w[:, _O_V:_O_V + MV_RANK].astype(BF16)


def _win_layout(w_in, layer):
    first = layer is None
    d, cols = w_in.shape[-2:]
    tr = 256
    if first:
        in_spec = pl.BlockSpec((tr, cols), lambda i: (i, 0))
    else:
        in_spec = pl.BlockSpec((None, tr, cols), lambda i: (layer, i, 0))
    out = lambda c: pl.BlockSpec((tr, c), lambda i: (i, 0))
    return pl.pallas_call(
        functools.partial(_win_layout_kernel, first=first),
        grid=(d // tr,),
        in_specs=[in_spec],
        out_specs=[out(3 * WIDTH), out(3 * WIDTH + SM_COLS)],
        out_shape=[jax.ShapeDtypeStruct((d, 3 * WIDTH), BF16),
                   jax.ShapeDtypeStruct((d, 3 * WIDTH + SM_COLS), BF16)],
        compiler_params=_params(("parallel",)),
        name="win_layout",
    )(w_in)


def _pad_cols(a, width):
    return jnp.pad(a, ((0, 0), (0, width - a.shape[1])))


def _pad_rows(a, height):
    return jnp.pad(a, ((0, height - a.shape[0]), (0, 0)))


def _layer_params(w_in, layer, mu, fox_fb, fox_out_g, w_up, w0, a_up, a0, g_up, v_up, v0,
                  k_k, k_a, r_k, lnx_g, lnx_b, ln1_g, ln1_b, ln2_g, ln2_b):
    first = v_up is None
    w_fox, w_rwkv = _win_layout(w_in, layer)
    o_w = 3 * WIDTH
    o_a = o_w + DECAY_RANK
    o_g = o_a + AAA_RANK
    o_v = o_g + GATE_RANK
    mus = [(jnp.zeros((1, HEADS), F32), SM_W - SM_F), (mu[None, o_w:o_a], SM_A - SM_W),
           (mu[None, o_a:o_g], SM_G - SM_A), (mu[None, o_g:o_v], SM_V - SM_G),
           (mu[None, o_v:], SM_COLS - SM_V)]
    mu_small = jnp.concatenate([_pad_cols(a, wd) for a, wd in mus], axis=1)
    row = lambda a: a.reshape(1, -1)
    lp = dict(
        w_fox=w_fox, w_rwkv=w_rwkv,
        mu_big=mu[None, 0:o_w], mu_small=mu_small,
        fb=_pad_cols(fox_fb[None, :], LANES),
        gains=fox_out_g.reshape(HEADS // 2, 1, 2 * HEAD_DIM),
        w_up=_pad_rows(w_up, SM_A - SM_W).astype(BF16), w0=row(w0),
        a_up=_pad_rows(a_up, SM_G - SM_A).astype(BF16), a0=row(a0),
        g_up=_pad_rows(g_up, SM_V - SM_G).astype(BF16),
        k_k=row(k_k), k_a=row(k_a), r_k=row(r_k), lnx_g=row(lnx_g), lnx_b=row(lnx_b),
        ln1_g=row(ln1_g), ln1_b=row(ln1_b), ln2_g=row(ln2_g), ln2_b=row(ln2_b),
    )
    if not first:
        lp["v_up"] = _pad_rows(v_up, SM_COLS - SM_V).astype(BF16)
        lp["v0"] = row(v0)
    return lp


def _layer(hf, hb, lp, layer, w_out, w_ff1, w_ff2, sel, selt, v_first, b, tp):
    qkv = _matmul(hb, lp["w_fox"], BF16, "inproj_fox")
    proj = _matmul(hb, lp["w_rwkv"], F32, "inproj_rwkv")
    qp, kp, vt = _fox_prep(qkv, proj, lp["fb"], b, tp)
    fox = _fox_attention(qp, kp, vt, lp["gains"], b, tp)
    outs = _rwkv_prep(proj, lp, sel, selt, v_first, tp)
    r, lw, k, v, kn, a, g, bg = outs[:8]
    if v_first is None:
        v_first = outs[8]
    rwkv = _rwkv_scan(r, lw, k, v, kn, a, g, bg, lp["lnx_g"], lp["lnx_b"], b, tp)
    xf, xb = _outproj_ln(fox, rwkv, hf, w_out, layer, lp["ln1_g"], lp["ln1_b"], tp)
    hf, hb = _ffn_ln(xb, xf, w_ff1, w_ff2, layer, lp["ln2_g"], lp["ln2_b"], tp)
    return hf, hb, v_first


def kernel(x, meta, ln_in_g, ln_in_b, w_in_first, w_in_rest, mu_first, mu_rest, fox_fb, fox_out_g, w_up, w0, a_up, a0, g_up, v_up, v0, k_k, k_a, r_k, lnx_g, lnx_b, w_out, ln1_g, ln1_b, w_ff1, w_ff2, ln2_g, ln2_b):
    b, seq, d = x.shape
    assert d == D_MODEL
    tp = FRONT_PAD + N_META + seq
    assert tp % LANES == 0
    head = jnp.concatenate([jnp.zeros((FRONT_PAD, d), x.dtype), meta.astype(x.dtype)], axis=0)
    h0 = jnp.concatenate([jnp.broadcast_to(head[None], (b, LANES, d)), x], axis=1).reshape(b * tp, d)
    hf, hb = _ln_rows(h0, ln_in_g, ln_in_b, tp)

    lane_head = jnp.arange(WIDTH, dtype=jnp.int32) // HEAD_DIM
    sel = (lane_head[:, None] == jnp.arange(LANES, dtype=jnp.int32)[None, :]).astype(BF16)
    selt = sel.T

    w_out_b, w_ff1_b, w_ff2_b = w_out.astype(BF16), w_ff1.astype(BF16), w_ff2.astype(BF16)

    v_first = None
    for l in range(DEPTH):
        first = l == 0
        lp = _layer_params(
            w_in_first if first else w_in_rest, None if first else l - 1,
            mu_first if first else mu_rest[l - 1],
            fox_fb[l], fox_out_g[l], w_up[l], w0[l], a_up[l], a0[l], g_up[l],
            None if first else v_up[l - 1], None if first else v0[l - 1],
            k_k[l], k_a[l], r_k[l], lnx_g[l], lnx_b[l],
            ln1_g[l], ln1_b[l], ln2_g[l], ln2_b[l])
        hf, hb, v_first = _layer(hf, hb, lp, l, w_out_b, w_ff1_b, w_ff2_b, sel, selt, v_first, b, tp)
    return hf.reshape(b, tp, d)[:, LANES:, :]
```

```python
import functools
import math

import jax
import jax.numpy as jnp
from jax import lax
from jax.experimental import pallas as pl
from jax.experimental.pallas import tpu as pltpu

F32 = jnp.float32
BF16 = jnp.bfloat16
HIGHEST = lax.Precision.HIGHEST

D_MODEL = 2048
DEPTH = 4
N_META = 16
LANES = 128
FRONT_PAD = (-N_META) % LANES
HEAD_DIM = 64
WIDTH = D_MODEL // 2
HEADS = WIDTH // HEAD_DIM
DECAY_RANK = 64
AAA_RANK = 64
GATE_RANK = 160
MV_RANK = 32
D_FF = 4 * D_MODEL
ALPHA = (2 * DEPTH) ** 0.25
LN_EPS = 1e-5
GN_EPS = 64e-5
OUT_NORM_EPS = 1e-6
DECAY_SCALE = math.exp(-0.5)
NEG_INF = -1e30
MASKED_KEY = 1e30
LOG2E = math.log2(math.e)
C_PARTS = 3
DENOM_LANE = (HEAD_DIM, HEAD_DIM - 1)
VT_ROWS = ((0, HEAD_DIM + 16), (HEAD_DIM - 16, LANES))

SM_F, SM_W, SM_A, SM_G, SM_V, SM_COLS = 0, 128, 256, 384, 640, 768

CHUNK = 64
SCAN_HEADS = 4
SCAN_W = SCAN_HEADS * HEAD_DIM
VMEM_LIMIT = 56 * 1024 * 1024


def _tile(total, cap, mult=64):
    best = None
    for d in range(mult, cap + 1, mult):
        if total % d == 0:
            best = d
    assert best is not None, (total, cap)
    return best


def _params(sem):
    return pltpu.CompilerParams(dimension_semantics=sem, vmem_limit_bytes=VMEM_LIMIT)


def _layer_norm(x, g, b):
    mean = jnp.mean(x, -1, keepdims=True)
    xc = x - mean
    var = jnp.mean(xc * xc, -1, keepdims=True)
    return xc * lax.rsqrt(var + LN_EPS) * g + b


def _ln_kernel(x_ref, g_ref, b_ref, of_ref, ob_ref):
    y = _layer_norm(x_ref[...], g_ref[...], b_ref[...])
    of_ref[...] = y
    ob_ref[...] = y.astype(BF16)


def _ln_rows(x, g, b, tp):
    n, d = x.shape
    tm = _tile(tp, 320)
    row = pl.BlockSpec((tm, d), lambda i: (i, 0))
    vec = pl.BlockSpec((1, d), lambda i: (0, 0))
    return pl.pallas_call(
        _ln_kernel,
        grid=(n // tm,),
        in_specs=[row, vec, vec],
        out_specs=[row, row],
        out_shape=[jax.ShapeDtypeStruct((n, d), F32), jax.ShapeDtypeStruct((n, d), BF16)],
        compiler_params=_params(("parallel",)),
        name="ln_in",
    )(x, g.reshape(1, d), b.reshape(1, d))


def _mm_kernel(x_ref, w_ref, o_ref):
    o_ref[...] = jnp.dot(x_ref[...], w_ref[...], preferred_element_type=F32).astype(o_ref.dtype)


def _matmul(x, w, out_dtype, name):
    n, k = x.shape
    nc = w.shape[1]
    tm = _tile(n, 1664)
    tn = _tile(nc, 768, LANES)
    return pl.pallas_call(
        _mm_kernel,
        grid=(n // tm, nc // tn),
        in_specs=[pl.BlockSpec((tm, k), lambda i, j: (i, 0)),
                  pl.BlockSpec((k, tn), lambda i, j: (0, j))],
        out_specs=pl.BlockSpec((tm, tn), lambda i, j: (i, j)),
        out_shape=jax.ShapeDtypeStruct((n, nc), out_dtype),
        compiler_params=_params(("parallel", "parallel")),
        name=name,
    )(x, w)


def _fox_prep_kernel(qkv_ref, f_ref, fb_ref, qo_ref, ko_ref, vo_ref, carry_ref, *, tm):
    t = pl.program_id(1)

    @pl.when(t == 0)
    def _():
        carry_ref[...] = jnp.zeros_like(carry_ref)

    z = f_ref[...] + fb_ref[...]
    lf = jnp.minimum(z, 0.0) - jnp.log1p(jnp.exp(-jnp.abs(z)))
    real = (t * tm + lax.broadcasted_iota(jnp.int32, (tm, LANES), 0)) >= FRONT_PAD
    lf = jnp.where(real, lf, 0.0)
    ri = lax.broadcasted_iota(jnp.int32, (tm, tm), 0)
    ci = lax.broadcasted_iota(jnp.int32, (tm, tm), 1)
    tri = (ci <= ri).astype(F32)
    c = jnp.dot(tri, lf, precision=HIGHEST, preferred_element_type=F32) + carry_ref[...]
    carry_ref[...] = c[tm - 1:tm, :]
    c = jnp.where(real, c, MASKED_KEY)

    lane = lax.broadcasted_iota(jnp.int32, (tm, LANES), 1)
    low = lane < HEAD_DIM
    cs = c * LOG2E
    hi = cs.astype(BF16).astype(F32)
    rest = cs - hi
    mid = rest.astype(BF16).astype(F32)
    lo = rest - mid
    for j in range(HEADS // 2):
        qb = qkv_ref[:, j * LANES:(j + 1) * LANES].astype(F32)
        kb = qkv_ref[:, WIDTH + j * LANES:WIDTH + (j + 1) * LANES].astype(F32)
        vb = qkv_ref[:, 2 * WIDTH + j * LANES:2 * WIDTH + (j + 1) * LANES].astype(F32)
        for hh in range(2):
            h = 2 * j + hh
            data = low if hh == 0 else jnp.logical_not(low)
            x0 = HEAD_DIM if hh == 0 else 0
            ones = jnp.where((lane >= x0) & (lane < x0 + C_PARTS), 1.0, 0.0)
            cpart = jnp.where(lane == x0, -hi[:, h:h + 1],
                              jnp.where(lane == x0 + 1, -mid[:, h:h + 1],
                                        jnp.where(lane == x0 + 2, -lo[:, h:h + 1], 0.0)))
            qo_ref[0, h] = jnp.where(data, qb, ones).astype(BF16)
            ko_ref[0, h] = jnp.where(data, kb, cpart).astype(BF16)
            vx = jnp.where(data, vb, jnp.where(lane == DENOM_LANE[hh], 1.0, 0.0))
            vo_ref[0, h] = vx.T.astype(BF16)


def _fox_prep(qkv, proj, fb_pad, b, tp):
    tm = _tile(tp, 640, LANES)
    nt = tp // tm
    f_block = (3 * WIDTH + SM_F) // LANES
    out = pl.BlockSpec((1, HEADS, tm, LANES), lambda bi, t: (bi, 0, t, 0))
    out_t = pl.BlockSpec((1, HEADS, LANES, tm), lambda bi, t: (bi, 0, 0, t))
    shape = jax.ShapeDtypeStruct((b, HEADS, tp, LANES), BF16)
    shape_t = jax.ShapeDtypeStruct((b, HEADS, LANES, tp), BF16)
    return pl.pallas_call(
        functools.partial(_fox_prep_kernel, tm=tm),
        grid=(b, nt),
        in_specs=[pl.BlockSpec((tm, 3 * WIDTH), lambda bi, t: (bi * nt + t, 0)),
                  pl.BlockSpec((tm, LANES), lambda bi, t: (bi * nt + t, f_block)),
                  pl.BlockSpec((1, LANES), lambda bi, t: (0, 0))],
        out_specs=[out, out, out_t],
        out_shape=[shape, shape, shape_t],
        scratch_shapes=[pltpu.VMEM((1, LANES), F32)],
        compiler_params=_params(("parallel", "arbitrary")),
        name="fox_prep",
    )(qkv, proj, fb_pad)


def _fox_kernel(q_ref, k_ref, v_ref, g_ref, o_ref, m_sc, acc_sc, sa_sc, sb_sc, *, blk):
    qi = pl.program_id(2)
    ri = lax.broadcasted_iota(jnp.int32, (blk, blk), 0)
    ci = lax.broadcasted_iota(jnp.int32, (blk, blk), 1)
    causal = ri <= ci
    for h in range(2):
        m_sc[h] = jnp.full((1, blk), NEG_INF, F32)
        acc_sc[h] = jnp.zeros((LANES, blk), F32)

    def key_rows(j):
        return pl.ds(pl.multiple_of(j * blk, LANES), blk)

    def scores(j, s_sc):
        for h in range(2):
            s_sc[h] = lax.dot_general(k_ref[0, h, key_rows(j), :], q_ref[0, h],
                                      (((1,), (1,)), ((), ())), preferred_element_type=F32)

    def consume(j, s_sc, masked):
        for h in range(2):
            s = s_sc[h]
            if masked:
                s = jnp.where(causal, s, NEG_INF)
            lo, hi = VT_ROWS[h]
            m_prev = m_sc[h]
            m_next = jnp.maximum(m_prev, jnp.max(s, axis=0, keepdims=True))
            p = jnp.exp2(s - m_next)
            acc_sc[h, lo:hi, :] = (jnp.exp2(m_prev - m_next) * acc_sc[h, lo:hi, :]
                                   + jnp.dot(v_ref[0, h, lo:hi, key_rows(j)], p.astype(BF16),
                                             preferred_element_type=F32))
            m_sc[h] = m_next

    scores(0, sa_sc)

    def pair(p, carry):
        j = 2 * p
        scores(j + 1, sb_sc)
        consume(j, sa_sc, False)
        scores(j + 2, sa_sc)
        consume(j + 1, sb_sc, False)
        return carry

    lax.fori_loop(0, qi // 2, pair, 0)

    @pl.when(qi % 2 == 1)
    def _():
        scores(qi, sb_sc)
        consume(qi - 1, sa_sc, False)
        consume(qi, sb_sc, True)

    @pl.when(qi % 2 == 0)
    def _():
        consume(qi, sa_sc, True)

    lane = lax.broadcasted_iota(jnp.int32, (blk, LANES), 1)
    low = lane < HEAD_DIM
    outs = []
    for h in range(2):
        data = low if h == 0 else jnp.logical_not(low)
        acc = acc_sc[h].T
        o = jnp.where(data, acc / acc[:, DENOM_LANE[h]:DENOM_LANE[h] + 1], 0.0)
        ms = jnp.sum(o * o, axis=1, keepdims=True) * (1.0 / HEAD_DIM)
        outs.append(o * lax.rsqrt(ms + OUT_NORM_EPS))
    o_ref[...] = ((outs[0] + outs[1]) * g_ref[0]).astype(o_ref.dtype)


def _fox_attention(qp, kp, vt, gains, b, tp):
    blk = _tile(tp, 640, LANES)
    nq = tp // blk
    hp = HEADS // 2
    kern = functools.partial(_fox_kernel, blk=blk)
    return pl.pallas_call(
        kern,
        grid=(b, hp, nq),
        in_specs=[pl.BlockSpec((1, 2, blk, LANES), lambda bi, h, qi: (bi, h, qi, 0)),
                  pl.BlockSpec((1, 2, tp, LANES), lambda bi, h, qi: (bi, h, 0, 0)),
                  pl.BlockSpec((1, 2, LANES, tp), lambda bi, h, qi: (bi, h, 0, 0)),
                  pl.BlockSpec((1, 1, LANES), lambda bi, h, qi: (h, 0, 0))],
        out_specs=pl.BlockSpec((blk, LANES), lambda bi, h, qi: (bi * nq + qi, h)),
        out_shape=jax.ShapeDtypeStruct((b * tp, WIDTH), BF16),
        scratch_shapes=[pltpu.VMEM((2, 1, blk), F32), pltpu.VMEM((2, LANES, blk), F32),
                        pltpu.VMEM((2, blk, blk), F32), pltpu.VMEM((2, blk, blk), F32)],
        compiler_params=_params(("parallel", "parallel", "arbitrary")),
        name="fox_attention",
    )(qp, kp, vt, gains)


def _split_dot(x, sel, parts):
    out = None
    for _ in range(parts):
        piece = x.astype(BF16)
        term = jnp.dot(piece, sel, preferred_element_type=F32)
        out = term if out is None else out + term
        x = x - piece.astype(F32)
    return out


def _head_sum(x, sel_ref, selt_ref):
    return _split_dot(_split_dot(x, sel_ref[...], 2), selt_ref[...], 2)


def _prep_kernel(*refs, tm, tiles_per_batch, first):
    if first:
        (big_ref, sm_ref, mub_ref, mus_ref, wup_ref, w0_ref, aup_ref, a0_ref, gup_ref,
         kk_ref, ka_ref, rk_ref, sel_ref, selt_ref,
         r_o, lw_o, k_o, v_o, kn_o, a_o, g_o, bg_o, vf_o, cb_sc, cs_sc) = refs
    else:
        (big_ref, sm_ref, mub_ref, mus_ref, wup_ref, w0_ref, aup_ref, a0_ref, gup_ref,
         vup_ref, v0_ref, vf_ref, kk_ref, ka_ref, rk_ref, sel_ref, selt_ref,
         r_o, lw_o, k_o, v_o, kn_o, a_o, g_o, bg_o, cb_sc, cs_sc) = refs
    i = pl.program_id(0)

    @pl.when(i == 0)
    def _():
        cb_sc[...] = jnp.zeros_like(cb_sc)
        cs_sc[...] = jnp.zeros_like(cs_sc)

    ri = lax.broadcasted_iota(jnp.int32, (tm, 1), 0)
    real = (lax.rem(i, tiles_per_batch) * tm + ri) >= FRONT_PAD

    def mix(x_ref, mu_ref, carry):
        h = jnp.where(real, x_ref[...], 0.0)
        prev = jnp.where(ri == 0, carry[...], pltpu.roll(h, 1, axis=0))
        carry[...] = h[tm - 1:tm, :]
        p = h + mu_ref[...] * (prev - h)
        return jnp.where(real, p, 0.0)

    pb = mix(big_ref, mub_ref, cb_sc)
    ps = mix(sm_ref, mus_ref, cs_sc)
    r = pb[:, 0:WIDTH]
    k = pb[:, WIDTH:2 * WIDTH]
    v = pb[:, 2 * WIDTH:3 * WIDTH]

    def lora(x, w_ref):
        return jnp.dot(x.astype(BF16), w_ref[...], preferred_element_type=F32)

    lw = -DECAY_SCALE * jax.nn.sigmoid(w0_ref[...] + lora(jnp.tanh(ps[:, SM_W:SM_A]), wup_ref))
    a = jax.nn.sigmoid(a0_ref[...] + lora(ps[:, SM_A:SM_G], aup_ref))
    g = lora(jax.nn.sigmoid(ps[:, SM_G:SM_V]), gup_ref)
    if first:
        vf_o[...] = v
    else:
        v = v + (vf_ref[...] - v) * jax.nn.sigmoid(v0_ref[...] + lora(ps[:, SM_V:SM_COLS], vup_ref))

    kn = k * kk_ref[...]
    ss = _split_dot(kn * kn, sel_ref[...], 2)
    inv = 1.0 / jnp.maximum(jnp.sqrt(ss), 1e-12)
    kn = kn * _split_dot(inv, selt_ref[...], 2)
    k = k * (1.0 + (a - 1.0) * ka_ref[...])
    bonus = _head_sum(r * k * rk_ref[...], sel_ref, selt_ref) * v

    lw_o[...] = lw
    r_o[...] = r.astype(BF16)
    k_o[...] = k.astype(BF16)
    v_o[...] = v.astype(BF16)
    kn_o[...] = kn.astype(BF16)
    a_o[...] = a.astype(BF16)
    g_o[...] = g.astype(BF16)
    bg_o[...] = (bonus * g).astype(BF16)


def _rwkv_prep(proj, lp, sel, selt, v_first, tp):
    n = proj.shape[0]
    first = v_first is None
    tm = _tile(tp, 320)
    row = lambda c: pl.BlockSpec((tm, c), lambda i: (i, 0))
    full = lambda a: pl.BlockSpec(a.shape, lambda i: (0,) * a.ndim)
    small_block = 3 * WIDTH // SM_COLS
    ins = [proj, proj, lp["mu_big"], lp["mu_small"], lp["w_up"], lp["w0"], lp["a_up"], lp["a0"], lp["g_up"]]
    specs = ([row(3 * WIDTH), pl.BlockSpec((tm, SM_COLS), lambda i: (i, small_block))]
             + [full(a) for a in ins[2:]])
    if not first:
        ins += [lp["v_up"], lp["v0"], v_first]
        specs += [full(lp["v_up"]), full(lp["v0"]), row(WIDTH)]
    tail = [lp["k_k"], lp["k_a"], lp["r_k"], sel, selt]
    ins += tail
    specs += [full(a) for a in tail]
    dtypes = [BF16, F32] + [BF16] * 6 + ([F32] if first else [])
    kern = functools.partial(_prep_kernel, tm=tm, tiles_per_batch=tp // tm, first=first)
    return pl.pallas_call(
        kern,
        grid=(n // tm,),
        in_specs=specs,
        out_specs=[row(WIDTH)] * len(dtypes),
        out_shape=[jax.ShapeDtypeStruct((n, WIDTH), dt) for dt in dtypes],
        scratch_shapes=[pltpu.VMEM((1, 3 * WIDTH), F32), pltpu.VMEM((1, SM_COLS), F32)],
        compiler_params=_params(("arbitrary",)),
        name="rwkv_prep",
    )(*ins)


def _scan_kernel(r_ref, lw_ref, k_ref, v_ref, kn_ref, a_ref, g_ref, bg_ref, lg_ref, lb_ref,
                 o_ref, s_sc, *, tt):
    ti = pl.program_id(1)

    @pl.when(ti == 0)
    def _():
        s_sc[...] = jnp.zeros_like(s_sc)

    w = SCAN_W
    ri = lax.broadcasted_iota(jnp.int32, (w, w), 0)
    ci = lax.broadcasted_iota(jnp.int32, (w, w), 1)
    same_head = (ri // HEAD_DIM) == (ci // HEAD_DIM)
    ones_bd = same_head.astype(BF16)
    rw = lax.broadcasted_iota(jnp.int32, (CHUNK, w), 0)
    sw = lax.broadcasted_iota(jnp.int32, (CHUNK, w), 1) % HEAD_DIM
    strict = sw < rw
    incl = sw <= rw
    eye = (sw == rw).astype(F32)
    tri_r = lax.broadcasted_iota(jnp.int32, (CHUNK, CHUNK), 0)
    tri_c = lax.broadcasted_iota(jnp.int32, (CHUNK, CHUNK), 1)
    tri = (tri_c <= tri_r).astype(BF16)

    def stack(x):
        return jnp.where(same_head, jnp.tile(x, (SCAN_HEADS, 1)), 0.0).astype(BF16)

    def rows2(x, y):
        return jnp.concatenate([x, y], axis=0).astype(BF16)

    def head_mean(x):
        hi = x.astype(BF16)
        both = jnp.dot(rows2(hi, x - hi.astype(F32)), ones_bd, preferred_element_type=F32)
        return (both[0:CHUNK] + both[CHUNK:2 * CHUNK]) * (1.0 / HEAD_DIM)

    def mm(x, y):
        return jnp.dot(x, y, preferred_element_type=F32)

    def mm_nt(x, y):
        return lax.dot_general(x, y, (((1,), (1,)), ((), ())), preferred_element_type=F32)

    def mm_tn(x, y):
        return lax.dot_general(x, y, (((0,), (0,)), ((), ())), preferred_element_type=F32)

    groups = range(WIDTH // w)

    def prepare(rows, gi):
        cols = slice(gi * w, (gi + 1) * w)
        lw = lw_ref[rows, cols]
        r, k, v, kn, a = (x[rows, cols].astype(F32) for x in (r_ref, k_ref, v_ref, kn_ref, a_ref))
        cum = None
        rest = lw
        for _ in range(3):
            piece = rest.astype(BF16)
            term = jnp.dot(tri, piece, preferred_element_type=F32)
            cum = term if cum is None else cum + term
            rest = rest - piece.astype(F32)
        p_in = jnp.exp(cum)
        p_ex = jnp.exp(cum - lw)
        p_inv = jnp.exp(-cum)
        p_end = p_in[CHUNK - 1:CHUNK, :]
        bt = kn * a * p_inv
        kt = k * p_inv
        return dict(ar=rows2(-kn * p_ex, r * p_in),
                    bk=jnp.concatenate([stack(bt), stack(kt)], axis=0),
                    v=v, vs=stack(v), p_end=p_end, bh_kh=rows2(bt * p_end, kt * p_end))

    def state_free(rows, out):
        g = [prepare(rows, gi) for gi in groups]
        gram = [mm_nt(x["ar"], x["bk"]) for x in g]
        yield
        a_ab = [jnp.where(strict, m[0:CHUNK, 0:w], 0.0) for m in gram]
        a_kv = [rows2(jnp.where(strict, m[0:CHUNK, w:2 * w], 0.0),
                      jnp.where(incl, m[CHUNK:2 * CHUNK, w:2 * w], 0.0)) for m in gram]
        a_rb = [jnp.where(incl, m[CHUNK:2 * CHUNK, 0:w], 0.0).astype(BF16) for m in gram]
        inv = [eye + m for m in a_ab]
        apow = [mm(m.astype(BF16), stack(m)) for m in a_ab]
        yield
        for _ in range(int(math.log2(CHUNK)) - 2):
            both = [mm(rows2(m, t), stack(m)) for m, t in zip(apow, inv)]
            inv = [t + b_[CHUNK:2 * CHUNK] for t, b_ in zip(inv, both)]
            apow = [b_[0:CHUNK] for b_ in both]
            yield
        inv = [t + mm(t.astype(BF16), stack(m)) for t, m in zip(inv, apow)]
        out.update(g=g, a_kv=a_kv, a_rb=a_rb, inv=[t.astype(BF16) for t in inv])

    def state_step(rows, pre, s0, s_out):
        g, a_kv, a_rb, inv = pre["g"], pre["a_kv"], pre["a_rb"], pre["inv"]
        xr = [mm_nt(x["ar"], s.astype(BF16)) for x, s in zip(g, s0)]
        av = [mm(m, x["vs"]) for m, x in zip(a_kv, g)]
        yield
        u = [mm(t, stack(x_[0:CHUNK] + v_[0:CHUNK])) for t, x_, v_ in zip(inv, xr, av)]
        yield
        for gi in groups:
            upd = mm_tn(rows2(u[gi], g[gi]["v"]), g[gi]["bh_kh"])
            s_out.append(s0[gi] * g[gi]["p_end"] + jnp.where(same_head, upd, 0.0))
        y = [x_[CHUNK:2 * CHUNK] + mm(m, stack(u_)) + v_[CHUNK:2 * CHUNK]
             for x_, m, u_, v_ in zip(xr, a_rb, u, av)]
        yield
        yc = [y_ - head_mean(y_) for y_ in y]
        yield
        var = [head_mean(c_ * c_) for c_ in yc]
        yield
        for gi in groups:
            cols = slice(gi * w, (gi + 1) * w)
            yn = yc[gi] * lax.rsqrt(var[gi] + GN_EPS) * lg_ref[:, cols] + lb_ref[:, cols]
            o_ref[rows, cols] = (yn * g_ref[rows, cols].astype(F32)
                                 + bg_ref[rows, cols].astype(F32)).astype(o_ref.dtype)

    def interleave(*gens):
        live = list(gens)
        while live:
            for gen in list(live):
                try:
                    next(gen)
                except StopIteration:
                    live.remove(gen)

    n_chunks = tt // CHUNK
    chunk_rows = [pl.ds(c * CHUNK, CHUNK) for c in range(n_chunks)]
    state = [s_sc[gi] for gi in groups]
    pre = {}
    interleave(state_free(chunk_rows[0], pre))
    for c in range(n_chunks):
        nxt, new_state = {}, []
        gens = [state_step(chunk_rows[c], pre, state, new_state)]
        if c + 1 < n_chunks:
            gens.insert(0, state_free(chunk_rows[c + 1], nxt))
        interleave(*gens)
        pre, state = nxt, new_state
    for gi in groups:
        s_sc[gi] = state[gi]


def _rwkv_scan(r, lw, k, v, kn, a, g, bg, lnx_g, lnx_b, b, tp):
    n = r.shape[0]
    tt = _tile(tp, 640)
    nt = tp // tt
    spec = pl.BlockSpec((tt, WIDTH), lambda bi, ti: (bi * nt + ti, 0))
    vec = pl.BlockSpec((1, WIDTH), lambda bi, ti: (0, 0))
    kern = functools.partial(_scan_kernel, tt=tt)
    return pl.pallas_call(
        kern,
        grid=(b, nt),
        in_specs=[spec] * 8 + [vec, vec],
        out_specs=spec,
        out_shape=jax.ShapeDtypeStruct((n, WIDTH), BF16),
        scratch_shapes=[pltpu.VMEM((WIDTH // SCAN_W, SCAN_W, SCAN_W), F32)],
        compiler_params=_params(("parallel", "arbitrary")),
        name="rwkv_scan",
    )(r, lw, k, v, kn, a, g, bg, lnx_g, lnx_b)


def _outproj_kernel(fox_ref, rwkv_ref, h_ref, wt_ref, wb_ref, g_ref, b_ref, of_ref, ob_ref):
    mixed = (jnp.dot(fox_ref[...], wt_ref[...], preferred_element_type=F32)
             + jnp.dot(rwkv_ref[...], wb_ref[...], preferred_element_type=F32))
    y = _layer_norm(ALPHA * h_ref[...] + mixed, g_ref[...], b_ref[...])
    of_ref[...] = y
    ob_ref[...] = y.astype(BF16)


def _outproj_ln(fox, rwkv, h, w_out, layer, g, b, tp):
    n, d = h.shape
    tm = _tile(tp, 320)
    half = pl.BlockSpec((tm, WIDTH), lambda i: (i, 0))
    row = pl.BlockSpec((tm, d), lambda i: (i, 0))
    vec = pl.BlockSpec((1, d), lambda i: (0, 0))
    return pl.pallas_call(
        _outproj_kernel,
        grid=(n // tm,),
        in_specs=[half, half, row,
                  pl.BlockSpec((None, WIDTH, d), lambda i: (layer, 0, 0)),
                  pl.BlockSpec((None, WIDTH, d), lambda i: (layer, 1, 0)),
                  vec, vec],
        out_specs=[row, row],
        out_shape=[jax.ShapeDtypeStruct((n, d), F32), jax.ShapeDtypeStruct((n, d), BF16)],
        compiler_params=_params(("parallel",)),
        name="outproj_ln",
    )(fox, rwkv, h, w_out, w_out, g, b)


def _ffn_kernel(xb_ref, xf_ref, w1_ref, w2_ref, g_ref, b_ref, of_ref, ob_ref, acc_sc):
    j = pl.program_id(1)

    @pl.when(j == 0)
    def _():
        acc_sc[...] = jnp.zeros_like(acc_sc)

    u = jnp.maximum(jnp.dot(xb_ref[...], w1_ref[...], preferred_element_type=F32), 0.0)
    acc_sc[...] += jnp.dot((u * u).astype(BF16), w2_ref[...], preferred_element_type=F32)

    @pl.when(j == pl.num_programs(1) - 1)
    def _():
        y = _layer_norm(ALPHA * xf_ref[...] + acc_sc[...], g_ref[...], b_ref[...])
        of_ref[...] = y
        ob_ref[...] = y.astype(BF16)


def _ffn_ln(xb, xf, w1, w2, layer, g, b, tp):
    n, d = xf.shape
    tm = _tile(tp, 640)
    tf = 512
    row = pl.BlockSpec((tm, d), lambda i, j: (i, 0))
    vec = pl.BlockSpec((1, d), lambda i, j: (0, 0))
    return pl.pallas_call(
        _ffn_kernel,
        grid=(n // tm, D_FF // tf),
        in_specs=[row, row,
                  pl.BlockSpec((None, d, tf), lambda i, j: (layer, 0, j)),
                  pl.BlockSpec((None, tf, d), lambda i, j: (layer, j, 0)),
                  vec, vec],
        out_specs=[row, row],
        out_shape=[jax.ShapeDtypeStruct((n, d), F32), jax.ShapeDtypeStruct((n, d), BF16)],
        scratch_shapes=[pltpu.VMEM((tm, d), F32)],
        compiler_params=_params(("parallel", "arbitrary")),
        name="ffn_ln",
    )(xb, xf, w1, w2, g, b)


_RS = 3 * WIDTH + HEADS
_O_W = _RS + 3 * WIDTH
_O_A = _O_W + DECAY_RANK
_O_G = _O_A + AAA_RANK
_O_V = _O_G + GATE_RANK


def _win_layout_kernel(w_ref, wf_ref, wr_ref, *, first):
    w = w_ref[...]
    lane = lax.broadcasted_iota(jnp.int32, (1, 3 * WIDTH), 1)
    col_scale = jnp.where(lane < WIDTH, HEAD_DIM ** -0.5 * LOG2E, 1.0)
    wf_ref[...] = (w[:, 0:3 * WIDTH] * col_scale).astype(BF16)
    s0 = 3 * WIDTH
    wr_ref[:, 0:s0] = w[:, _RS:_O_W].astype(BF16)
    wr_ref[:, s0:s0 + SM_COLS] = jnp.zeros((w.shape[0], SM_COLS), BF16)
    wr_ref[:, s0 + SM_F:s0 + SM_F + HEADS] = w[:, 3 * WIDTH:_RS].astype(BF16)
    wr_ref[:, s0 + SM_W:s0 + SM_W + DECAY_RANK] = w[:, _O_W:_O_A].astype(BF16)
    wr_ref[:, s0 + SM_A:s0 + SM_A + AAA_RANK] = w[:, _O_A:_O_G].astype(BF16)
    wr_ref[:, s0 + SM_G:s0 + SM_G + GATE_RANK] = w[:, _O_G:_O_V].astype(BF16)
    if not first:
        wr_ref[:, s0 + SM_V:s0 + SM_V + MV_RANK] = w[:, _O_V:_O_V + MV_RANK].astype(BF16)


def _win_layout(w_in, layer):
    first = layer is None
    d, cols = w_in.shape[-2:]
    tr = 256
    if first:
        in_spec = pl.BlockSpec((tr, cols), lambda i: (i, 0))
    else:
        in_spec = pl.BlockSpec((None, tr, cols), lambda i: (layer, i, 0))
    out = lambda c: pl.BlockSpec((tr, c), lambda i: (i, 0))
    return pl.pallas_call(
        functools.partial(_win_layout_kernel, first=first),
        grid=(d // tr,),
        in_specs=[in_spec],
        out_specs=[out(3 * WIDTH), out(3 * WIDTH + SM_COLS)],
        out_shape=[jax.ShapeDtypeStruct((d, 3 * WIDTH), BF16),
                   jax.ShapeDtypeStruct((d, 3 * WIDTH + SM_COLS), BF16)],
        compiler_params=_params(("parallel",)),
        name="win_layout",
    )(w_in)


def _pad_cols(a, width):
    return jnp.pad(a, ((0, 0), (0, width - a.shape[1])))


def _pad_rows(a, height):
    return jnp.pad(a, ((0, height - a.shape[0]), (0, 0)))


def _layer_params(w_in, layer, mu, fox_fb, fox_out_g, w_up, w0, a_up, a0, g_up, v_up, v0,
                  k_k, k_a, r_k, lnx_g, lnx_b, ln1_g, ln1_b, ln2_g, ln2_b):
    first = v_up is None
    w_fox, w_rwkv = _win_layout(w_in, layer)
    o_w = 3 * WIDTH
    o_a = o_w + DECAY_RANK
    o_g = o_a + AAA_RANK
    o_v = o_g + GATE_RANK
    mus = [(jnp.zeros((1, HEADS), F32), SM_W - SM_F), (mu[None, o_w:o_a], SM_A - SM_W),
           (mu[None, o_a:o_g], SM_G - SM_A), (mu[None, o_g:o_v], SM_V - SM_G),
           (mu[None, o_v:], SM_COLS - SM_V)]
    mu_small = jnp.concatenate([_pad_cols(a, wd) for a, wd in mus], axis=1)
    row = lambda a: a.reshape(1, -1)
    lp = dict(
        w_fox=w_fox, w_rwkv=w_rwkv,
        mu_big=mu[None, 0:o_w], mu_small=mu_small,
        fb=_pad_cols(fox_fb[None, :], LANES),
        gains=fox_out_g.reshape(HEADS // 2, 1, 2 * HEAD_DIM),
        w_up=_pad_rows(w_up, SM_A - SM_W).astype(BF16), w0=row(w0),
        a_up=_pad_rows(a_up, SM_G - SM_A).astype(BF16), a0=row(a0),
        g_up=_pad_rows(g_up, SM_V - SM_G).astype(BF16),
        k_k=row(k_k), k_a=row(k_a), r_k=row(r_k), lnx_g=row(lnx_g), lnx_b=row(lnx_b),
        ln1_g=row(ln1_g), ln1_b=row(ln1_b), ln2_g=row(ln2_g), ln2_b=row(ln2_b),
    )
    if not first:
        lp["v_up"] = _pad_rows(v_up, SM_COLS - SM_V).astype(BF16)
        lp["v0"] = row(v0)
    return lp


def _layer(hf, hb, lp, layer, w_out, w_ff1, w_ff2, sel, selt, v_first, b, tp):
    qkv = _matmul(hb, lp["w_fox"], BF16, "inproj_fox")
    proj = _matmul(hb, lp["w_rwkv"], F32, "inproj_rwkv")
    qp, kp, vt = _fox_prep(qkv, proj, lp["fb"], b, tp)
    fox = _fox_attention(qp, kp, vt, lp["gains"], b, tp)
    outs = _rwkv_prep(proj, lp, sel, selt, v_first, tp)
    r, lw, k, v, kn, a, g, bg = outs[:8]
    if v_first is None:
        v_first = outs[8]
    rwkv = _rwkv_scan(r, lw, k, v, kn, a, g, bg, lp["lnx_g"], lp["lnx_b"], b, tp)
    xf, xb = _outproj_ln(fox, rwkv, hf, w_out, layer, lp["ln1_g"], lp["ln1_b"], tp)
    hf, hb = _ffn_ln(xb, xf, w_ff1, w_ff2, layer, lp["ln2_g"], lp["ln2_b"], tp)
    return hf, hb, v_first


def kernel(x, meta, ln_in_g, ln_in_b, w_in_first, w_in_rest, mu_first, mu_rest, fox_fb, fox_out_g, w_up, w0, a_up, a0, g_up, v_up, v0, k_k, k_a, r_k, lnx_g, lnx_b, w_out, ln1_g, ln1_b, w_ff1, w_ff2, ln2_g, ln2_b):
    b, seq, d = x.shape
    assert d == D_MODEL
    tp = FRONT_PAD + N_META + seq
    assert tp % LANES == 0
    head = jnp.concatenate([jnp.zeros((FRONT_PAD, d), x.dtype), meta.astype(x.dtype)], axis=0)
    h0 = jnp.concatenate([jnp.broadcast_to(head[None], (b, LANES, d)), x], axis=1).reshape(b * tp, d)
    hf, hb = _ln_rows(h0, ln_in_g, ln_in_b, tp)

    lane_head = jnp.arange(WIDTH, dtype=jnp.int32) // HEAD_DIM
    sel = (lane_head[:, None] == jnp.arange(LANES, dtype=jnp.int32)[None, :]).astype(BF16)
    selt = sel.T

    w_out_b, w_ff1_b, w_ff2_b = w_out.astype(BF16), w_ff1.astype(BF16), w_ff2.astype(BF16)

    v_first = None
    for l in range(DEPTH):
        first = l == 0
        lp = _layer_params(
            w_in_first if first else w_in_rest, None if first else l - 1,
            mu_first if first else mu_rest[l - 1],
            fox_fb[l], fox_out_g[l], w_up[l], w0[l], a_up[l], a0[l], g_up[l],
            None if first else v_up[l - 1], None if first else v0[l - 1],
            k_k[l], k_a[l], r_k[l], lnx_g[l], lnx_b[l],
            ln1_g[l], ln1_b[l], ln2_g[l], ln2_b[l])
        hf, hb, v_first = _layer(hf, hb, lp, l, w_out_b, w_ff1_b, w_ff2_b, sel, selt, v_first, b, tp)
    return hf.reshape(b, tp, d)[:, LANES:, :]
```

```python
import functools
import math

import jax
import jax.numpy as jnp
from jax import lax
from jax.experimental import pallas as pl
from jax.experimental.pallas import tpu as pltpu

F32 = jnp.float32
BF16 = jnp.bfloat16
HIGHEST = lax.Precision.HIGHEST

D_MODEL = 2048
DEPTH = 4
N_META = 16
LANES = 128
FRONT_PAD = (-N_META) % LANES
X_START = FRONT_PAD + N_META
HEAD_DIM = 64
WIDTH = D_MODEL // 2
HEADS = WIDTH // HEAD_DIM
DECAY_RANK = 64
AAA_RANK = 64
GATE_RANK = 160
MV_RANK = 32
D_FF = 4 * D_MODEL
ALPHA = (2 * DEPTH) ** 0.25
LN_EPS = 1e-5
GN_EPS = 64e-5
OUT_NORM_EPS = 1e-6
DECAY_SCALE = math.exp(-0.5)
NEG_INF = -1e30
MASKED_KEY = 1e30
LOG2E = math.log2(math.e)
C_PARTS = 3
DENOM_LANE = (HEAD_DIM, HEAD_DIM - 1)
VT_ROWS = ((0, HEAD_DIM + 16), (HEAD_DIM - 16, LANES))

SM_F, SM_W, SM_A, SM_G, SM_V, SM_COLS = 0, 128, 256, 384, 640, 768

CHUNK = 64
SCAN_HEADS = 4
SCAN_W = SCAN_HEADS * HEAD_DIM
VMEM_LIMIT = 56 * 1024 * 1024


def _tile(total, cap, mult=64):
    best = None
    for d in range(mult, cap + 1, mult):
        if total % d == 0:
            best = d
    assert best is not None, (total, cap)
    return best


def _params(sem):
    return pltpu.CompilerParams(dimension_semantics=sem, vmem_limit_bytes=VMEM_LIMIT)


def _layer_norm(x, g, b):
    mean = jnp.mean(x, -1, keepdims=True)
    xc = x - mean
    var = jnp.mean(xc * xc, -1, keepdims=True)
    return xc * lax.rsqrt(var + LN_EPS) * g + b


def _ln_kernel(x_ref, g_ref, b_ref, of_ref, ob_ref):
    y = _layer_norm(x_ref[...], g_ref[...], b_ref[...])
    of_ref[...] = y
    ob_ref[...] = y.astype(BF16)


def _ln_rows(x, g, b, tp):
    n, d = x.shape
    tm = _tile(tp, 320)
    row = pl.BlockSpec((tm, d), lambda i: (i, 0))
    vec = pl.BlockSpec((1, d), lambda i: (0, 0))
    return pl.pallas_call(
        _ln_kernel,
        grid=(n // tm,),
        in_specs=[row, vec, vec],
        out_specs=[row, row],
        out_shape=[jax.ShapeDtypeStruct((n, d), F32), jax.ShapeDtypeStruct((n, d), BF16)],
        compiler_params=_params(("parallel",)),
        name="ln_in",
    )(x, g.reshape(1, d), b.reshape(1, d))


def _mm_kernel(x_ref, w_ref, o_ref):
    o_ref[...] = jnp.dot(x_ref[...], w_ref[...], preferred_element_type=F32).astype(o_ref.dtype)


def _matmul(x, w, out_dtype, name):
    n, k = x.shape
    nc = w.shape[1]
    tm = _tile(n, 1664)
    tn = _tile(nc, 768, LANES)
    return pl.pallas_call(
        _mm_kernel,
        grid=(n // tm, nc // tn),
        in_specs=[pl.BlockSpec((tm, k), lambda i, j: (i, 0)),
                  pl.BlockSpec((k, tn), lambda i, j: (0, j))],
        out_specs=pl.BlockSpec((tm, tn), lambda i, j: (i, j)),
        out_shape=jax.ShapeDtypeStruct((n, nc), out_dtype),
        compiler_params=_params(("parallel", "parallel")),
        name=name,
    )(x, w)


def _fox_prep_kernel(qkv_ref, f_ref, fb_ref, qo_ref, ko_ref, vo_ref, carry_ref, *, tm):
    t = pl.program_id(1)

    @pl.when(t == 0)
    def _():
        carry_ref[...] = jnp.zeros_like(carry_ref)

    z = f_ref[...] + fb_ref[...]
    lf = jnp.minimum(z, 0.0) - jnp.log1p(jnp.exp(-jnp.abs(z)))
    real = (t * tm + lax.broadcasted_iota(jnp.int32, (tm, LANES), 0)) >= FRONT_PAD
    lf = jnp.where(real, lf, 0.0)
    ri = lax.broadcasted_iota(jnp.int32, (tm, tm), 0)
    ci = lax.broadcasted_iota(jnp.int32, (tm, tm), 1)
    tri = (ci <= ri).astype(F32)
    c = jnp.dot(tri, lf, precision=HIGHEST, preferred_element_type=F32) + carry_ref[...]
    carry_ref[...] = c[tm - 1:tm, :]
    c = jnp.where(real, c, MASKED_KEY)

    lane = lax.broadcasted_iota(jnp.int32, (tm, LANES), 1)
    low = lane < HEAD_DIM
    cs = c * LOG2E
    hi = cs.astype(BF16).astype(F32)
    rest = cs - hi
    mid = rest.astype(BF16).astype(F32)
    lo = rest - mid
    for j in range(HEADS // 2):
        qb = qkv_ref[:, j * LANES:(j + 1) * LANES].astype(F32)
        kb = qkv_ref[:, WIDTH + j * LANES:WIDTH + (j + 1) * LANES].astype(F32)
        vb = qkv_ref[:, 2 * WIDTH + j * LANES:2 * WIDTH + (j + 1) * LANES].astype(F32)
        for hh in range(2):
            h = 2 * j + hh
            data = low if hh == 0 else jnp.logical_not(low)
            x0 = HEAD_DIM if hh == 0 else 0
            ones = jnp.where((lane >= x0) & (lane < x0 + C_PARTS), 1.0, 0.0)
            cpart = jnp.where(lane == x0, -hi[:, h:h + 1],
                              jnp.where(lane == x0 + 1, -mid[:, h:h + 1],
                                        jnp.where(lane == x0 + 2, -lo[:, h:h + 1], 0.0)))
            qo_ref[0, h] = jnp.where(data, qb, ones).astype(BF16)
            ko_ref[0, h] = jnp.where(data, kb, cpart).astype(BF16)
            vx = jnp.where(data, vb, jnp.where(lane == DENOM_LANE[hh], 1.0, 0.0))
            vo_ref[0, h] = vx.T.astype(BF16)


def _fox_prep(qkv, proj, fb_pad, b, tp):
    tm = _tile(tp, 640, LANES)
    nt = tp // tm
    f_block = (3 * WIDTH + SM_F) // LANES
    out = pl.BlockSpec((1, HEADS, tm, LANES), lambda bi, t: (bi, 0, t, 0))
    out_t = pl.BlockSpec((1, HEADS, LANES, tm), lambda bi, t: (bi, 0, 0, t))
    shape = jax.ShapeDtypeStruct((b, HEADS, tp, LANES), BF16)
    shape_t = jax.ShapeDtypeStruct((b, HEADS, LANES, tp), BF16)
    return pl.pallas_call(
        functools.partial(_fox_prep_kernel, tm=tm),
        grid=(b, nt),
        in_specs=[pl.BlockSpec((tm, 3 * WIDTH), lambda bi, t: (bi * nt + t, 0)),
                  pl.BlockSpec((tm, LANES), lambda bi, t: (bi * nt + t, f_block)),
                  pl.BlockSpec((1, LANES), lambda bi, t: (0, 0))],
        out_specs=[out, out, out_t],
        out_shape=[shape, shape, shape_t],
        scratch_shapes=[pltpu.VMEM((1, LANES), F32)],
        compiler_params=_params(("parallel", "arbitrary")),
        name="fox_prep",
    )(qkv, proj, fb_pad)


def _fox_kernel(q_ref, k_ref, v_ref, g_ref, o_ref, m_sc, acc_sc, sa_sc, sb_sc, xa_sc, xb_sc, *, blk):
    qi = pl.program_id(2)
    ri = lax.broadcasted_iota(jnp.int32, (blk, blk), 0)
    ci = lax.broadcasted_iota(jnp.int32, (blk, blk), 1)
    causal = ri <= ci
    for h in range(2):
        m_sc[h] = jnp.full((1, blk), NEG_INF, F32)
        acc_sc[h] = jnp.zeros((LANES, blk), F32)

    def key_rows(j):
        return pl.ds(pl.multiple_of(j * blk, LANES), blk)

    slot_a, slot_b = (sa_sc, xa_sc), (sb_sc, xb_sc)

    def scores(j, slot, masked):
        s_sc, mx_sc = slot
        for h in range(2):
            s = lax.dot_general(k_ref[0, h, key_rows(j), :], q_ref[0, h],
                                (((1,), (1,)), ((), ())), preferred_element_type=F32)
            if masked:
                s = jnp.where(causal, s, NEG_INF)
            s_sc[h] = s
            mx_sc[h] = jnp.max(s, axis=0, keepdims=True)

    def consume(j, slot):
        s_sc, mx_sc = slot
        for h in range(2):
            lo, hi = VT_ROWS[h]
            m_prev = m_sc[h]
            m_next = jnp.maximum(m_prev, mx_sc[h])
            p = jnp.exp2(s_sc[h] - m_next)
            acc_sc[h, lo:hi, :] = (jnp.exp2(m_prev - m_next) * acc_sc[h, lo:hi, :]
                                   + jnp.dot(v_ref[0, h, lo:hi, key_rows(j)], p.astype(BF16),
                                             preferred_element_type=F32))
            m_sc[h] = m_next

    def pair(j, last_masked):
        scores(j + 1, slot_b, False)
        consume(j, slot_a)
        scores(j + 2, slot_a, last_masked)
        consume(j + 1, slot_b)

    @pl.when(qi == 0)
    def _():
        scores(0, slot_a, True)

    @pl.when(qi != 0)
    def _():
        scores(0, slot_a, False)

    even = qi % 2 == 0
    n_loop = jnp.where(even, jnp.maximum(qi // 2 - 1, 0), qi // 2)

    def body(p, carry):
        pair(2 * p, False)
        return carry

    lax.fori_loop(0, n_loop, body, 0)

    @pl.when(jnp.logical_and(even, qi >= 2))
    def _():
        pair(qi - 2, True)

    @pl.when(even)
    def _():
        consume(qi, slot_a)

    @pl.when(jnp.logical_not(even))
    def _():
        scores(qi, slot_b, True)
        consume(qi - 1, slot_a)
        consume(qi, slot_b)

    lane = lax.broadcasted_iota(jnp.int32, (blk, LANES), 1)
    low = lane < HEAD_DIM
    outs = []
    for h in range(2):
        data = low if h == 0 else jnp.logical_not(low)
        acc = acc_sc[h].T
        o = jnp.where(data, acc / acc[:, DENOM_LANE[h]:DENOM_LANE[h] + 1], 0.0)
        ms = jnp.sum(o * o, axis=1, keepdims=True) * (1.0 / HEAD_DIM)
        outs.append(o * lax.rsqrt(ms + OUT_NORM_EPS))
    o_ref[...] = ((outs[0] + outs[1]) * g_ref[0]).astype(o_ref.dtype)


def _fox_attention(qp, kp, vt, gains, b, tp):
    blk = _tile(tp, 640, LANES)
    nq = tp // blk
    hp = HEADS // 2
    kern = functools.partial(_fox_kernel, blk=blk)
    return pl.pallas_call(
        kern,
        grid=(b, hp, nq),
        in_specs=[pl.BlockSpec((1, 2, blk, LANES), lambda bi, h, qi: (bi, h, qi, 0)),
                  pl.BlockSpec((1, 2, tp, LANES), lambda bi, h, qi: (bi, h, 0, 0)),
                  pl.BlockSpec((1, 2, LANES, tp), lambda bi, h, qi: (bi, h, 0, 0)),
                  pl.BlockSpec((1, 1, LANES), lambda bi, h, qi: (h, 0, 0))],
        out_specs=pl.BlockSpec((blk, LANES), lambda bi, h, qi: (bi * nq + qi, h)),
        out_shape=jax.ShapeDtypeStruct((b * tp, WIDTH), BF16),
        scratch_shapes=[pltpu.VMEM((2, 1, blk), F32), pltpu.VMEM((2, LANES, blk), F32),
                        pltpu.VMEM((2, blk, blk), F32), pltpu.VMEM((2, blk, blk), F32),
                        pltpu.VMEM((2, 1, blk), F32), pltpu.VMEM((2, 1, blk), F32)],
        compiler_params=_params(("parallel", "parallel", "arbitrary")),
        name="fox_attention",
    )(qp, kp, vt, gains)


def _split_dot(x, sel, parts):
    out = None
    for _ in range(parts):
        piece = x.astype(BF16)
        term = jnp.dot(piece, sel, preferred_element_type=F32)
        out = term if out is None else out + term
        x = x - piece.astype(F32)
    return out


def _head_sum(x, sel_ref, selt_ref):
    return _split_dot(_split_dot(x, sel_ref[...], 2), selt_ref[...], 2)


def _prep_kernel(*refs, tm, tiles_per_batch, first):
    if first:
        (big_ref, sm_ref, mub_ref, mus_ref, wup_ref, w0_ref, aup_ref, a0_ref, gup_ref,
         kk_ref, ka_ref, rk_ref, sel_ref, selt_ref,
         r_o, lw_o, k_o, v_o, kn_o, a_o, g_o, bg_o, vf_o, cb_sc, cs_sc) = refs
    else:
        (big_ref, sm_ref, mub_ref, mus_ref, wup_ref, w0_ref, aup_ref, a0_ref, gup_ref,
         vup_ref, v0_ref, vf_ref, kk_ref, ka_ref, rk_ref, sel_ref, selt_ref,
         r_o, lw_o, k_o, v_o, kn_o, a_o, g_o, bg_o, cb_sc, cs_sc) = refs
    i = pl.program_id(0)

    @pl.when(i == 0)
    def _():
        cb_sc[...] = jnp.zeros_like(cb_sc)
        cs_sc[...] = jnp.zeros_like(cs_sc)

    ri = lax.broadcasted_iota(jnp.int32, (tm, 1), 0)
    real = (lax.rem(i, tiles_per_batch) * tm + ri) >= FRONT_PAD

    def mix(x_ref, mu_ref, carry):
        h = jnp.where(real, x_ref[...], 0.0)
        prev = jnp.where(ri == 0, carry[...], pltpu.roll(h, 1, axis=0))
        carry[...] = h[tm - 1:tm, :]
        p = h + mu_ref[...] * (prev - h)
        return jnp.where(real, p, 0.0)

    pb = mix(big_ref, mub_ref, cb_sc)
    ps = mix(sm_ref, mus_ref, cs_sc)
    r = pb[:, 0:WIDTH]
    k = pb[:, WIDTH:2 * WIDTH]
    v = pb[:, 2 * WIDTH:3 * WIDTH]

    def lora(x, w_ref):
        return jnp.dot(x.astype(BF16), w_ref[...], preferred_element_type=F32)

    lw = -DECAY_SCALE * jax.nn.sigmoid(w0_ref[...] + lora(jnp.tanh(ps[:, SM_W:SM_A]), wup_ref))
    a = jax.nn.sigmoid(a0_ref[...] + lora(ps[:, SM_A:SM_G], aup_ref))
    g = lora(jax.nn.sigmoid(ps[:, SM_G:SM_V]), gup_ref)
    if first:
        vf_o[...] = v
    else:
        v = v + (vf_ref[...] - v) * jax.nn.sigmoid(v0_ref[...] + lora(ps[:, SM_V:SM_COLS], vup_ref))

    kn = k * kk_ref[...]
    ss = _split_dot(kn * kn, sel_ref[...], 2)
    inv = 1.0 / jnp.maximum(jnp.sqrt(ss), 1e-12)
    kn = kn * _split_dot(inv, selt_ref[...], 2)
    k = k * (1.0 + (a - 1.0) * ka_ref[...])
    bonus = _head_sum(r * k * rk_ref[...], sel_ref, selt_ref) * v

    lw_o[...] = lw
    r_o[...] = r.astype(BF16)
    k_o[...] = k.astype(BF16)
    v_o[...] = v.astype(BF16)
    kn_o[...] = kn.astype(BF16)
    a_o[...] = a.astype(BF16)
    g_o[...] = g.astype(BF16)
    bg_o[...] = (bonus * g).astype(BF16)


def _rwkv_prep(proj, lp, sel, selt, v_first, tp):
    n = proj.shape[0]
    first = v_first is None
    tm = _tile(tp, 320)
    row = lambda c: pl.BlockSpec((tm, c), lambda i: (i, 0))
    full = lambda a: pl.BlockSpec(a.shape, lambda i: (0,) * a.ndim)
    small_block = 3 * WIDTH // SM_COLS
    ins = [proj, proj, lp["mu_big"], lp["mu_small"], lp["w_up"], lp["w0"], lp["a_up"], lp["a0"], lp["g_up"]]
    specs = ([row(3 * WIDTH), pl.BlockSpec((tm, SM_COLS), lambda i: (i, small_block))]
             + [full(a) for a in ins[2:]])
    if not first:
        ins += [lp["v_up"], lp["v0"], v_first]
        specs += [full(lp["v_up"]), full(lp["v0"]), row(WIDTH)]
    tail = [lp["k_k"], lp["k_a"], lp["r_k"], sel, selt]
    ins += tail
    specs += [full(a) for a in tail]
    dtypes = [BF16, F32] + [BF16] * 6 + ([F32] if first else [])
    kern = functools.partial(_prep_kernel, tm=tm, tiles_per_batch=tp // tm, first=first)
    return pl.pallas_call(
        kern,
        grid=(n // tm,),
        in_specs=specs,
        out_specs=[row(WIDTH)] * len(dtypes),
        out_shape=[jax.ShapeDtypeStruct((n, WIDTH), dt) for dt in dtypes],
        scratch_shapes=[pltpu.VMEM((1, 3 * WIDTH), F32), pltpu.VMEM((1, SM_COLS), F32)],
        compiler_params=_params(("arbitrary",)),
        name="rwkv_prep",
    )(*ins)


def _scan_kernel(r_ref, lw_ref, k_ref, v_ref, kn_ref, a_ref, g_ref, bg_ref, lg_ref, lb_ref,
                 o_ref, s_sc, *, tt):
    ti = pl.program_id(1)

    @pl.when(ti == 0)
    def _():
        s_sc[...] = jnp.zeros_like(s_sc)

    w = SCAN_W
    ri = lax.broadcasted_iota(jnp.int32, (w, w), 0)
    ci = lax.broadcasted_iota(jnp.int32, (w, w), 1)
    same_head = (ri // HEAD_DIM) == (ci // HEAD_DIM)
    ones_bd = same_head.astype(BF16)
    rw = lax.broadcasted_iota(jnp.int32, (CHUNK, w), 0)
    sw = lax.broadcasted_iota(jnp.int32, (CHUNK, w), 1) % HEAD_DIM
    strict = sw < rw
    incl = sw <= rw
    eye = (sw == rw).astype(F32)
    tri_r = lax.broadcasted_iota(jnp.int32, (CHUNK, CHUNK), 0)
    tri_c = lax.broadcasted_iota(jnp.int32, (CHUNK, CHUNK), 1)
    tri = (tri_c <= tri_r).astype(BF16)

    def stack(x):
        return jnp.where(same_head, jnp.tile(x, (SCAN_HEADS, 1)), 0.0).astype(BF16)

    def rows2(x, y):
        return jnp.concatenate([x, y], axis=0).astype(BF16)

    def head_mean(x):
        hi = x.astype(BF16)
        both = jnp.dot(rows2(hi, x - hi.astype(F32)), ones_bd, preferred_element_type=F32)
        return (both[0:CHUNK] + both[CHUNK:2 * CHUNK]) * (1.0 / HEAD_DIM)

    def mm(x, y):
        return jnp.dot(x, y, preferred_element_type=F32)

    def mm_nt(x, y):
        return lax.dot_general(x, y, (((1,), (1,)), ((), ())), preferred_element_type=F32)

    def mm_tn(x, y):
        return lax.dot_general(x, y, (((0,), (0,)), ((), ())), preferred_element_type=F32)

    groups = range(WIDTH // w)

    def prepare(rows, gi):
        cols = slice(gi * w, (gi + 1) * w)
        lw = lw_ref[rows, cols]
        r, k, v, kn, a = (x[rows, cols].astype(F32) for x in (r_ref, k_ref, v_ref, kn_ref, a_ref))
        cum = None
        rest = lw
        for _ in range(3):
            piece = rest.astype(BF16)
            term = jnp.dot(tri, piece, preferred_element_type=F32)
            cum = term if cum is None else cum + term
            rest = rest - piece.astype(F32)
        p_in = jnp.exp(cum)
        p_ex = jnp.exp(cum - lw)
        p_inv = jnp.exp(-cum)
        p_end = p_in[CHUNK - 1:CHUNK, :]
        bt = kn * a * p_inv
        kt = k * p_inv
        return dict(ar=rows2(-kn * p_ex, r * p_in),
                    bk=jnp.concatenate([stack(bt), stack(kt)], axis=0),
                    v=v, vs=stack(v), p_end=p_end, bh_kh=rows2(bt * p_end, kt * p_end))

    def state_free(rows, out):
        g = [prepare(rows, gi) for gi in groups]
        gram = [mm_nt(x["ar"], x["bk"]) for x in g]
        yield
        a_ab = [jnp.where(strict, m[0:CHUNK, 0:w], 0.0) for m in gram]
        a_kv = [rows2(jnp.where(strict, m[0:CHUNK, w:2 * w], 0.0),
                      jnp.where(incl, m[CHUNK:2 * CHUNK, w:2 * w], 0.0)) for m in gram]
        a_rb = [jnp.where(incl, m[CHUNK:2 * CHUNK, 0:w], 0.0).astype(BF16) for m in gram]
        inv = [eye + m for m in a_ab]
        apow = [mm(m.astype(BF16), stack(m)) for m in a_ab]
        yield
        for _ in range(int(math.log2(CHUNK)) - 2):
            both = [mm(rows2(m, t), stack(m)) for m, t in zip(apow, inv)]
            inv = [t + b_[CHUNK:2 * CHUNK] for t, b_ in zip(inv, both)]
            apow = [b_[0:CHUNK] for b_ in both]
            yield
        inv = [t + mm(t.astype(BF16), stack(m)) for t, m in zip(inv, apow)]
        out.update(g=g, a_kv=a_kv, a_rb=a_rb, inv=[t.astype(BF16) for t in inv])

    def state_step(rows, pre, s0, s_out):
        g, a_kv, a_rb, inv = pre["g"], pre["a_kv"], pre["a_rb"], pre["inv"]
        xr = [mm_nt(x["ar"], s.astype(BF16)) for x, s in zip(g, s0)]
        av = [mm(m, x["vs"]) for m, x in zip(a_kv, g)]
        yield
        u = [mm(t, stack(x_[0:CHUNK] + v_[0:CHUNK])) for t, x_, v_ in zip(inv, xr, av)]
        yield
        for gi in groups:
            upd = mm_tn(rows2(u[gi], g[gi]["v"]), g[gi]["bh_kh"])
            s_out.append(s0[gi] * g[gi]["p_end"] + jnp.where(same_head, upd, 0.0))
        y = [x_[CHUNK:2 * CHUNK] + mm(m, stack(u_)) + v_[CHUNK:2 * CHUNK]
             for x_, m, u_, v_ in zip(xr, a_rb, u, av)]
        yield
        yc = [y_ - head_mean(y_) for y_ in y]
        yield
        var = [head_mean(c_ * c_) for c_ in yc]
        yield
        for gi in groups:
            cols = slice(gi * w, (gi + 1) * w)
            yn = yc[gi] * lax.rsqrt(var[gi] + GN_EPS) * lg_ref[:, cols] + lb_ref[:, cols]
            o_ref[rows, cols] = (yn * g_ref[rows, cols].astype(F32)
                                 + bg_ref[rows, cols].astype(F32)).astype(o_ref.dtype)

    def interleave(*gens):
        live = list(gens)
        while live:
            for gen in list(live):
                try:
                    next(gen)
                except StopIteration:
                    live.remove(gen)

    n_chunks = tt // CHUNK
    chunk_rows = [pl.ds(c * CHUNK, CHUNK) for c in range(n_chunks)]
    state = [s_sc[gi] for gi in groups]
    pre = {}
    interleave(state_free(chunk_rows[0], pre))
    for c in range(n_chunks):
        nxt, new_state = {}, []
        gens = [state_step(chunk_rows[c], pre, state, new_state)]
        if c + 1 < n_chunks:
            gens.insert(0, state_free(chunk_rows[c + 1], nxt))
        interleave(*gens)
        pre, state = nxt, new_state
    for gi in groups:
        s_sc[gi] = state[gi]


def _rwkv_scan(r, lw, k, v, kn, a, g, bg, lnx_g, lnx_b, b, tp):
    n = r.shape[0]
    tt = _tile(tp, 640)
    nt = tp // tt
    spec = pl.BlockSpec((tt, WIDTH), lambda bi, ti: (bi * nt + ti, 0))
    vec = pl.BlockSpec((1, WIDTH), lambda bi, ti: (0, 0))
    kern = functools.partial(_scan_kernel, tt=tt)
    return pl.pallas_call(
        kern,
        grid=(b, nt),
        in_specs=[spec] * 8 + [vec, vec],
        out_specs=spec,
        out_shape=jax.ShapeDtypeStruct((n, WIDTH), BF16),
        scratch_shapes=[pltpu.VMEM((WIDTH // SCAN_W, SCAN_W, SCAN_W), F32)],
        compiler_params=_params(("parallel", "arbitrary")),
        name="rwkv_scan",
    )(r, lw, k, v, kn, a, g, bg, lnx_g, lnx_b)


def _outproj_kernel(fox_ref, rwkv_ref, h_ref, wt_ref, wb_ref, g_ref, b_ref, of_ref, ob_ref):
    mixed = (jnp.dot(fox_ref[...], wt_ref[...], preferred_element_type=F32)
             + jnp.dot(rwkv_ref[...], wb_ref[...], preferred_element_type=F32))
    y = _layer_norm(ALPHA * h_ref[...] + mixed, g_ref[...], b_ref[...])
    of_ref[...] = y
    ob_ref[...] = y.astype(BF16)


def _outproj_ln(fox, rwkv, h, w_out, layer, g, b, tp):
    n, d = h.shape
    tm = _tile(tp, 320)
    half = pl.BlockSpec((tm, WIDTH), lambda i: (i, 0))
    row = pl.BlockSpec((tm, d), lambda i: (i, 0))
    vec = pl.BlockSpec((1, d), lambda i: (0, 0))
    return pl.pallas_call(
        _outproj_kernel,
        grid=(n // tm,),
        in_specs=[half, half, row,
                  pl.BlockSpec((None, WIDTH, d), lambda i: (layer, 0, 0)),
                  pl.BlockSpec((None, WIDTH, d), lambda i: (layer, 1, 0)),
                  vec, vec],
        out_specs=[row, row],
        out_shape=[jax.ShapeDtypeStruct((n, d), F32), jax.ShapeDtypeStruct((n, d), BF16)],
        compiler_params=_params(("parallel",)),
        name="outproj_ln",
    )(fox, rwkv, h, w_out, w_out, g, b)


def _ffn_kernel(xb_ref, xf_ref, w1_ref, w2_ref, g_ref, b_ref, of_ref, ob_ref, acc_sc):
    j = pl.program_id(1)

    @pl.when(j == 0)
    def _():
        acc_sc[...] = jnp.zeros_like(acc_sc)

    u = jnp.maximum(jnp.dot(xb_ref[...], w1_ref[...], preferred_element_type=F32), 0.0)
    acc_sc[...] += jnp.dot((u * u).astype(BF16), w2_ref[...], preferred_element_type=F32)

    @pl.when(j == pl.num_programs(1) - 1)
    def _():
        y = _layer_norm(ALPHA * xf_ref[...] + acc_sc[...], g_ref[...], b_ref[...])
        of_ref[...] = y
        ob_ref[...] = y.astype(BF16)


def _ffn_ln(xb, xf, w1, w2, layer, g, b, tp):
    n, d = xf.shape
    tm = _tile(tp, 640)
    tf = 1024
    row = pl.BlockSpec((tm, d), lambda i, j: (i, 0))
    vec = pl.BlockSpec((1, d), lambda i, j: (0, 0))
    return pl.pallas_call(
        _ffn_kernel,
        grid=(n // tm, D_FF // tf),
        in_specs=[row, row,
                  pl.BlockSpec((None, d, tf), lambda i, j: (layer, 0, j)),
                  pl.BlockSpec((None, tf, d), lambda i, j: (layer, j, 0)),
                  vec, vec],
        out_specs=[pl.BlockSpec((tm, d), lambda i, j: (i, 0), pipeline_mode=pl.Buffered(1)),
                   pl.BlockSpec((tm, d), lambda i, j: (i, 0), pipeline_mode=pl.Buffered(1))],
        out_shape=[jax.ShapeDtypeStruct((n, d), F32), jax.ShapeDtypeStruct((n, d), BF16)],
        scratch_shapes=[pltpu.VMEM((tm, d), F32)],
        compiler_params=_params(("parallel", "arbitrary")),
        name="ffn_ln",
    )(xb, xf, w1, w2, g, b)


_RS = 3 * WIDTH + HEADS
_O_W = _RS + 3 * WIDTH
_O_A = _O_W + DECAY_RANK
_O_G = _O_A + AAA_RANK
_O_V = _O_G + GATE_RANK


def _win_layout_kernel(w_ref, wf_ref, wr_ref, *, first):
    w = w_ref[...]
    lane = lax.broadcasted_iota(jnp.int32, (1, 3 * WIDTH), 1)
    col_scale = jnp.where(lane < WIDTH, HEAD_DIM ** -0.5 * LOG2E, 1.0)
    wf_ref[...] = (w[:, 0:3 * WIDTH] * col_scale).astype(BF16)
    s0 = 3 * WIDTH
    wr_ref[:, 0:s0] = w[:, _RS:_O_W].astype(BF16)
    wr_ref[:, s0:s0 + SM_COLS] = jnp.zeros((w.shape[0], SM_COLS), BF16)
    wr_ref[:, s0 + SM_F:s0 + SM_F + HEADS] = w[:, 3 * WIDTH:_RS].astype(BF16)
    wr_ref[:, s0 + SM_W:s0 + SM_W + DECAY_RANK] = w[:, _O_W:_O_A].astype(BF16)
    wr_ref[:, s0 + SM_A:s0 + SM_A + AAA_RANK] = w[:, _O_A:_O_G].astype(BF16)
    wr_ref[:, s0 + SM_G:s0 + SM_G + GATE_RANK] = w[:, _O_G:_O_V].astype(BF16)
    if not first:
        wr_ref[:, s0 + SM_V:s0 + SM_V + MV_RANK] = w[:, _O_V:_O_V + MV_RANK].astype(BF16)


def _win_layout(w_in, layer):
    first = layer is None
    d, cols = w_in.shape[-2:]
    tr = 256
    if first:
        in_spec = pl.BlockSpec((tr, cols), lambda i: (i, 0))
    else:
        in_spec = pl.BlockSpec((None, tr, cols), lambda i: (layer, i, 0))
    out = lambda c: pl.BlockSpec((tr, c), lambda i: (i, 0))
    return pl.pallas_call(
        functools.partial(_win_layout_kernel, first=first),
        grid=(d // tr,),
        in_specs=[in_spec],
        out_specs=[out(3 * WIDTH), out(3 * WIDTH + SM_COLS)],
        out_shape=[jax.ShapeDtypeStruct((d, 3 * WIDTH), BF16),
                   jax.ShapeDtypeStruct((d, 3 * WIDTH + SM_COLS), BF16)],
        compiler_params=_params(("parallel",)),
        name="win_layout",
    )(w_in)


def _pad_cols(a, width):
    return jnp.pad(a, ((0, 0), (0, width - a.shape[1])))


def _pad_rows(a, height):
    return jnp.pad(a, ((0, height - a.shape[0]), (0, 0)))


def _layer_params(w_in, layer, mu, fox_fb, fox_out_g, w_up, w0, a_up, a0, g_up, v_up, v0,
                  k_k, k_a, r_k, lnx_g, lnx_b, ln1_g, ln1_b, ln2_g, ln2_b):
    first = v_up is None
    w_fox, w_rwkv = _win_layout(w_in, layer)
    o_w = 3 * WIDTH
    o_a = o_w + DECAY_RANK
    o_g = o_a + AAA_RANK
    o_v = o_g + GATE_RANK
    mus = [(jnp.zeros((1, HEADS), F32), SM_W - SM_F), (mu[None, o_w:o_a], SM_A - SM_W),
           (mu[None, o_a:o_g], SM_G - SM_A), (mu[None, o_g:o_v], SM_V - SM_G),
           (mu[None, o_v:], SM_COLS - SM_V)]
    mu_small = jnp.concatenate([_pad_cols(a, wd) for a, wd in mus], axis=1)
    row = lambda a: a.reshape(1, -1)
    lp = dict(
        w_fox=w_fox, w_rwkv=w_rwkv,
        mu_big=mu[None, 0:o_w], mu_small=mu_small,
        fb=_pad_cols(fox_fb[None, :], LANES),
        gains=fox_out_g.reshape(HEADS // 2, 1, 2 * HEAD_DIM),
        w_up=_pad_rows(w_up, SM_A - SM_W).astype(BF16), w0=row(w0),
        a_up=_pad_rows(a_up, SM_G - SM_A).astype(BF16), a0=row(a0),
        g_up=_pad_rows(g_up, SM_V - SM_G).astype(BF16),
        k_k=row(k_k), k_a=row(k_a), r_k=row(r_k), lnx_g=row(lnx_g), lnx_b=row(lnx_b),
        ln1_g=row(ln1_g), ln1_b=row(ln1_b), ln2_g=row(ln2_g), ln2_b=row(ln2_b),
    )
    if not first:
        lp["v_up"] = _pad_rows(v_up, SM_COLS - SM_V).astype(BF16)
        lp["v0"] = row(v0)
    return lp


def _layer(hf, hb, lp, layer, w_out, w_ff1, w_ff2, sel, selt, v_first, b, tp):
    qkv = _matmul(hb, lp["w_fox"], BF16, "inproj_fox")
    proj = _matmul(hb, lp["w_rwkv"], F32, "inproj_rwkv")
    qp, kp, vt = _fox_prep(qkv, proj, lp["fb"], b, tp)
    fox = _fox_attention(qp, kp, vt, lp["gains"], b, tp)
    outs = _rwkv_prep(proj, lp, sel, selt, v_first, tp)
    r, lw, k, v, kn, a, g, bg = outs[:8]
    if v_first is None:
        v_first = outs[8]
    rwkv = _rwkv_scan(r, lw, k, v, kn, a, g, bg, lp["lnx_g"], lp["lnx_b"], b, tp)
    xf, xb = _outproj_ln(fox, rwkv, hf, w_out, layer, lp["ln1_g"], lp["ln1_b"], tp)
    hf, hb = _ffn_ln(xb, xf, w_ff1, w_ff2, layer, lp["ln2_g"], lp["ln2_b"], tp)
    return hf, hb, v_first


def kernel(x, meta, ln_in_g, ln_in_b, w_in_first, w_in_rest, mu_first, mu_rest, fox_fb, fox_out_g, w_up, w0, a_up, a0, g_up, v_up, v0, k_k, k_a, r_k, lnx_g, lnx_b, w_out, ln1_g, ln1_b, w_ff1, w_ff2, ln2_g, ln2_b):
    b, seq, d = x.shape
    assert d == D_MODEL
    tp = FRONT_PAD + N_META + seq
    assert tp % LANES == 0
    head = jnp.concatenate([jnp.zeros((FRONT_PAD, d), x.dtype), meta.astype(x.dtype)], axis=0)
    h0 = jnp.concatenate([jnp.broadcast_to(head[None], (b, X_START, d)), x], axis=1).reshape(b * tp, d)
    hf, hb = _ln_rows(h0, ln_in_g, ln_in_b, tp)

    lane_head = jnp.arange(WIDTH, dtype=jnp.int32) // HEAD_DIM
    sel = (lane_head[:, None] == jnp.arange(LANES, dtype=jnp.int32)[None, :]).astype(BF16)
    selt = sel.T

    w_out_b, w_ff1_b, w_ff2_b = w_out.astype(BF16), w_ff1.astype(BF16), w_ff2.astype(BF16)

    v_first = None
    for l in range(DEPTH):
        first = l == 0
        lp = _layer_params(
            w_in_first if first else w_in_rest, None if first else l - 1,
            mu_first if first else mu_rest[l - 1],
            fox_fb[l], fox_out_g[l], w_up[l], w0[l], a_up[l], a0[l], g_up[l],
            None if first else v_up[l - 1], None if first else v0[l - 1],
            k_k[l], k_a[l], r_k[l], lnx_g[l], lnx_b[l],
            ln1_g[l], ln1_b[l], ln2_g[l], ln2_b[l])
        hf, hb, v_first = _layer(hf, hb, lp, l, w_out_b, w_ff1_b, w_ff2_b, sel, selt, v_first, b, tp)
    return hf.reshape(b, tp, d)[:, X_START:, :]
```

```python
import functools
import math

import jax
import jax.numpy as jnp
from jax import lax
from jax.experimental import pallas as pl
from jax.experimental.pallas import tpu as pltpu

F32 = jnp.float32
BF16 = jnp.bfloat16
HIGHEST = lax.Precision.HIGHEST

D_MODEL = 2048
DEPTH = 4
N_META = 16
LANES = 128
FRONT_PAD = (-N_META) % LANES
X_START = FRONT_PAD + N_META
HEAD_DIM = 64
WIDTH = D_MODEL // 2
HEADS = WIDTH // HEAD_DIM
DECAY_RANK = 64
AAA_RANK = 64
GATE_RANK = 160
MV_RANK = 32
D_FF = 4 * D_MODEL
ALPHA = (2 * DEPTH) ** 0.25
LN_EPS = 1e-5
GN_EPS = 64e-5
OUT_NORM_EPS = 1e-6
DECAY_SCALE = math.exp(-0.5)
NEG_INF = -1e30
MASKED_KEY = 1e30
LOG2E = math.log2(math.e)
C_PARTS = 3
DENOM_LANE = (HEAD_DIM, HEAD_DIM - 1)
VT_ROWS = ((0, HEAD_DIM + 16), (HEAD_DIM - 16, LANES))

SM_F, SM_W, SM_A, SM_G, SM_V, SM_COLS = 0, 128, 256, 384, 640, 768

CHUNK = 64
SCAN_HEADS = 4
SCAN_W = SCAN_HEADS * HEAD_DIM
VMEM_LIMIT = 56 * 1024 * 1024


def _tile(total, cap, mult=64):
    best = None
    for d in range(mult, cap + 1, mult):
        if total % d == 0:
            best = d
    assert best is not None, (total, cap)
    return best


def _params(sem):
    return pltpu.CompilerParams(dimension_semantics=sem, vmem_limit_bytes=VMEM_LIMIT)


def _layer_norm(x, g, b):
    mean = jnp.mean(x, -1, keepdims=True)
    xc = x - mean
    var = jnp.mean(xc * xc, -1, keepdims=True)
    return xc * lax.rsqrt(var + LN_EPS) * g + b


def _ln_kernel(x_ref, g_ref, b_ref, of_ref, ob_ref):
    y = _layer_norm(x_ref[...], g_ref[...], b_ref[...])
    of_ref[...] = y
    ob_ref[...] = y.astype(BF16)


def _ln_rows(x, g, b, tp):
    n, d = x.shape
    tm = _tile(tp, 320)
    row = pl.BlockSpec((tm, d), lambda i: (i, 0))
    vec = pl.BlockSpec((1, d), lambda i: (0, 0))
    return pl.pallas_call(
        _ln_kernel,
        grid=(n // tm,),
        in_specs=[row, vec, vec],
        out_specs=[row, row],
        out_shape=[jax.ShapeDtypeStruct((n, d), F32), jax.ShapeDtypeStruct((n, d), BF16)],
        compiler_params=_params(("parallel",)),
        name="ln_in",
    )(x, g.reshape(1, d), b.reshape(1, d))


def _mm_kernel(x_ref, w_ref, o_ref):
    o_ref[...] = jnp.dot(x_ref[...], w_ref[...], preferred_element_type=F32).astype(o_ref.dtype)


def _matmul(x, w, out_dtype, name):
    n, k = x.shape
    nc = w.shape[1]
    tm = _tile(n, 1664)
    tn = _tile(nc, 768, LANES)
    return pl.pallas_call(
        _mm_kernel,
        grid=(n // tm, nc // tn),
        in_specs=[pl.BlockSpec((tm, k), lambda i, j: (i, 0)),
                  pl.BlockSpec((k, tn), lambda i, j: (0, j))],
        out_specs=pl.BlockSpec((tm, tn), lambda i, j: (i, j)),
        out_shape=jax.ShapeDtypeStruct((n, nc), out_dtype),
        compiler_params=_params(("parallel", "parallel")),
        name=name,
    )(x, w)


def _fox_prep_kernel(qkv_ref, f_ref, fb_ref, qo_ref, ko_ref, vo_ref, carry_ref, *, tm):
    t = pl.program_id(1)

    @pl.when(t == 0)
    def _():
        carry_ref[...] = jnp.zeros_like(carry_ref)

    z = f_ref[...] + fb_ref[...]
    lf = jnp.minimum(z, 0.0) - jnp.log1p(jnp.exp(-jnp.abs(z)))
    real = (t * tm + lax.broadcasted_iota(jnp.int32, (tm, LANES), 0)) >= FRONT_PAD
    lf = jnp.where(real, lf, 0.0)
    ri = lax.broadcasted_iota(jnp.int32, (tm, tm), 0)
    ci = lax.broadcasted_iota(jnp.int32, (tm, tm), 1)
    tri = (ci <= ri).astype(F32)
    c = jnp.dot(tri, lf, precision=HIGHEST, preferred_element_type=F32) + carry_ref[...]
    carry_ref[...] = c[tm - 1:tm, :]
    c = jnp.where(real, c, MASKED_KEY)

    lane = lax.broadcasted_iota(jnp.int32, (tm, LANES), 1)
    low = lane < HEAD_DIM
    cs = c * LOG2E
    hi = cs.astype(BF16).astype(F32)
    rest = cs - hi
    mid = rest.astype(BF16).astype(F32)
    lo = rest - mid
    for j in range(HEADS // 2):
        qb = qkv_ref[:, j * LANES:(j + 1) * LANES].astype(F32)
        kb = qkv_ref[:, WIDTH + j * LANES:WIDTH + (j + 1) * LANES].astype(F32)
        vb = qkv_ref[:, 2 * WIDTH + j * LANES:2 * WIDTH + (j + 1) * LANES].astype(F32)
        for hh in range(2):
            h = 2 * j + hh
            data = low if hh == 0 else jnp.logical_not(low)
            x0 = HEAD_DIM if hh == 0 else 0
            ones = jnp.where((lane >= x0) & (lane < x0 + C_PARTS), 1.0, 0.0)
            cpart = jnp.where(lane == x0, -hi[:, h:h + 1],
                              jnp.where(lane == x0 + 1, -mid[:, h:h + 1],
                                        jnp.where(lane == x0 + 2, -lo[:, h:h + 1], 0.0)))
            qo_ref[0, h] = jnp.where(data, qb, ones).astype(BF16)
            ko_ref[0, h] = jnp.where(data, kb, cpart).astype(BF16)
            vx = jnp.where(data, vb, jnp.where(lane == DENOM_LANE[hh], 1.0, 0.0))
            vo_ref[0, h] = vx.T.astype(BF16)


def _fox_prep(qkv, proj, fb_pad, b, tp):
    tm = _tile(tp, 640, LANES)
    nt = tp // tm
    f_block = (3 * WIDTH + SM_F) // LANES
    out = pl.BlockSpec((1, HEADS, tm, LANES), lambda bi, t: (bi, 0, t, 0))
    out_t = pl.BlockSpec((1, HEADS, LANES, tm), lambda bi, t: (bi, 0, 0, t))
    shape = jax.ShapeDtypeStruct((b, HEADS, tp, LANES), BF16)
    shape_t = jax.ShapeDtypeStruct((b, HEADS, LANES, tp), BF16)
    return pl.pallas_call(
        functools.partial(_fox_prep_kernel, tm=tm),
        grid=(b, nt),
        in_specs=[pl.BlockSpec((tm, 3 * WIDTH), lambda bi, t: (bi * nt + t, 0)),
                  pl.BlockSpec((tm, LANES), lambda bi, t: (bi * nt + t, f_block)),
                  pl.BlockSpec((1, LANES), lambda bi, t: (0, 0))],
        out_specs=[out, out, out_t],
        out_shape=[shape, shape, shape_t],
        scratch_shapes=[pltpu.VMEM((1, LANES), F32)],
        compiler_params=_params(("parallel", "arbitrary")),
        name="fox_prep",
    )(qkv, proj, fb_pad)


def _fox_kernel(q_ref, k_ref, v_ref, g_ref, o_ref, m_sc, acc_sc, sa_sc, sb_sc, xa_sc, xb_sc, *, blk):
    qi = pl.program_id(2)
    ri = lax.broadcasted_iota(jnp.int32, (blk, blk), 0)
    ci = lax.broadcasted_iota(jnp.int32, (blk, blk), 1)
    causal = ri <= ci
    for h in range(2):
        m_sc[h] = jnp.full((1, blk), NEG_INF, F32)
        acc_sc[h] = jnp.zeros((LANES, blk), F32)

    def key_rows(j):
        return pl.ds(pl.multiple_of(j * blk, LANES), blk)

    slot_a, slot_b = (sa_sc, xa_sc), (sb_sc, xb_sc)

    def scores(j, slot, masked):
        s_sc, mx_sc = slot
        for h in range(2):
            s = lax.dot_general(k_ref[0, h, key_rows(j), :], q_ref[0, h],
                                (((1,), (1,)), ((), ())), preferred_element_type=F32)
            if masked:
                s = jnp.where(causal, s, NEG_INF)
            s_sc[h] = s
            mx_sc[h] = jnp.max(s, axis=0, keepdims=True)

    def consume(j, slot):
        s_sc, mx_sc = slot
        for h in range(2):
            lo, hi = VT_ROWS[h]
            m_prev = m_sc[h]
            m_next = jnp.maximum(m_prev, mx_sc[h])
            p = jnp.exp2(s_sc[h] - m_next)
            acc_sc[h, lo:hi, :] = (jnp.exp2(m_prev - m_next) * acc_sc[h, lo:hi, :]
                                   + jnp.dot(v_ref[0, h, lo:hi, key_rows(j)], p.astype(BF16),
                                             preferred_element_type=F32))
            m_sc[h] = m_next

    def pair(j, last_masked):
        scores(j + 1, slot_b, False)
        consume(j, slot_a)
        scores(j + 2, slot_a, last_masked)
        consume(j + 1, slot_b)

    @pl.when(qi == 0)
    def _():
        scores(0, slot_a, True)

    @pl.when(qi != 0)
    def _():
        scores(0, slot_a, False)

    even = qi % 2 == 0
    n_loop = jnp.where(even, jnp.maximum(qi // 2 - 1, 0), qi // 2)

    def body(p, carry):
        pair(2 * p, False)
        return carry

    lax.fori_loop(0, n_loop, body, 0)

    @pl.when(jnp.logical_and(even, qi >= 2))
    def _():
        pair(qi - 2, True)

    @pl.when(even)
    def _():
        consume(qi, slot_a)

    @pl.when(jnp.logical_not(even))
    def _():
        scores(qi, slot_b, True)
        consume(qi - 1, slot_a)
        consume(qi, slot_b)

    lane = lax.broadcasted_iota(jnp.int32, (blk, LANES), 1)
    low = lane < HEAD_DIM
    outs = []
    for h in range(2):
        data = low if h == 0 else jnp.logical_not(low)
        acc = acc_sc[h].T
        o = jnp.where(data, acc / acc[:, DENOM_LANE[h]:DENOM_LANE[h] + 1], 0.0)
        ms = jnp.sum(o * o, axis=1, keepdims=True) * (1.0 / HEAD_DIM)
        outs.append(o * lax.rsqrt(ms + OUT_NORM_EPS))
    o_ref[...] = ((outs[0] + outs[1]) * g_ref[0]).astype(o_ref.dtype)


def _fox_attention(qp, kp, vt, gains, b, tp):
    blk = _tile(tp, 640, LANES)
    nq = tp // blk
    hp = HEADS // 2
    kern = functools.partial(_fox_kernel, blk=blk)
    return pl.pallas_call(
        kern,
        grid=(b, hp, nq),
        in_specs=[pl.BlockSpec((1, 2, blk, LANES), lambda bi, h, qi: (bi, h, qi, 0)),
                  pl.BlockSpec((1, 2, tp, LANES), lambda bi, h, qi: (bi, h, 0, 0)),
                  pl.BlockSpec((1, 2, LANES, tp), lambda bi, h, qi: (bi, h, 0, 0)),
                  pl.BlockSpec((1, 1, LANES), lambda bi, h, qi: (h, 0, 0))],
        out_specs=pl.BlockSpec((blk, LANES), lambda bi, h, qi: (bi * nq + qi, h)),
        out_shape=jax.ShapeDtypeStruct((b * tp, WIDTH), BF16),
        scratch_shapes=[pltpu.VMEM((2, 1, blk), F32), pltpu.VMEM((2, LANES, blk), F32),
                        pltpu.VMEM((2, blk, blk), F32), pltpu.VMEM((2, blk, blk), F32),
                        pltpu.VMEM((2, 1, blk), F32), pltpu.VMEM((2, 1, blk), F32)],
        compiler_params=_params(("parallel", "parallel", "arbitrary")),
        name="fox_attention",
    )(qp, kp, vt, gains)


def _split_dot(x, sel, parts):
    out = None
    for _ in range(parts):
        piece = x.astype(BF16)
        term = jnp.dot(piece, sel, preferred_element_type=F32)
        out = term if out is None else out + term
        x = x - piece.astype(F32)
    return out


def _head_sum(x, sel_ref, selt_ref):
    return _split_dot(_split_dot(x, sel_ref[...], 2), selt_ref[...], 2)


def _prep_kernel(*refs, tm, tiles_per_batch, first):
    if first:
        (big_ref, sm_ref, mub_ref, mus_ref, wup_ref, w0_ref, aup_ref, a0_ref, gup_ref,
         kk_ref, ka_ref, rk_ref, sel_ref, selt_ref,
         r_o, lw_o, k_o, v_o, kn_o, a_o, g_o, bg_o, vf_o, cb_sc, cs_sc) = refs
    else:
        (big_ref, sm_ref, mub_ref, mus_ref, wup_ref, w0_ref, aup_ref, a0_ref, gup_ref,
         vup_ref, v0_ref, vf_ref, kk_ref, ka_ref, rk_ref, sel_ref, selt_ref,
         r_o, lw_o, k_o, v_o, kn_o, a_o, g_o, bg_o, cb_sc, cs_sc) = refs
    i = pl.program_id(0)

    @pl.when(i == 0)
    def _():
        cb_sc[...] = jnp.zeros_like(cb_sc)
        cs_sc[...] = jnp.zeros_like(cs_sc)

    ri = lax.broadcasted_iota(jnp.int32, (tm, 1), 0)
    real = (lax.rem(i, tiles_per_batch) * tm + ri) >= FRONT_PAD

    def mix(x_ref, mu_ref, carry):
        h = jnp.where(real, x_ref[...], 0.0)
        prev = jnp.where(ri == 0, carry[...], pltpu.roll(h, 1, axis=0))
        carry[...] = h[tm - 1:tm, :]
        p = h + mu_ref[...] * (prev - h)
        return jnp.where(real, p, 0.0)

    pb = mix(big_ref, mub_ref, cb_sc)
    ps = mix(sm_ref, mus_ref, cs_sc)
    r = pb[:, 0:WIDTH]
    k = pb[:, WIDTH:2 * WIDTH]
    v = pb[:, 2 * WIDTH:3 * WIDTH]

    def lora(x, w_ref):
        return jnp.dot(x.astype(BF16), w_ref[...], preferred_element_type=F32)

    lw = -DECAY_SCALE * jax.nn.sigmoid(w0_ref[...] + lora(jnp.tanh(ps[:, SM_W:SM_A]), wup_ref))
    a = jax.nn.sigmoid(a0_ref[...] + lora(ps[:, SM_A:SM_G], aup_ref))
    g = lora(jax.nn.sigmoid(ps[:, SM_G:SM_V]), gup_ref)
    if first:
        vf_o[...] = v
    else:
        v = v + (vf_ref[...] - v) * jax.nn.sigmoid(v0_ref[...] + lora(ps[:, SM_V:SM_COLS], vup_ref))

    kn = k * kk_ref[...]
    ss = _split_dot(kn * kn, sel_ref[...], 2)
    inv = 1.0 / jnp.maximum(jnp.sqrt(ss), 1e-12)
    kn = kn * _split_dot(inv, selt_ref[...], 2)
    k = k * (1.0 + (a - 1.0) * ka_ref[...])
    bonus = _head_sum(r * k * rk_ref[...], sel_ref, selt_ref) * v

    lw_o[...] = lw
    r_o[...] = r.astype(BF16)
    k_o[...] = k.astype(BF16)
    v_o[...] = v.astype(BF16)
    kn_o[...] = kn.astype(BF16)
    a_o[...] = a.astype(BF16)
    g_o[...] = g.astype(BF16)
    bg_o[...] = (bonus * g).astype(BF16)


def _rwkv_prep(proj, lp, sel, selt, v_first, tp):
    n = proj.shape[0]
    first = v_first is None
    tm = _tile(tp, 320)
    row = lambda c: pl.BlockSpec((tm, c), lambda i: (i, 0))
    full = lambda a: pl.BlockSpec(a.shape, lambda i: (0,) * a.ndim)
    small_block = 3 * WIDTH // SM_COLS
    ins = [proj, proj, lp["mu_big"], lp["mu_small"], lp["w_up"], lp["w0"], lp["a_up"], lp["a0"], lp["g_up"]]
    specs = ([row(3 * WIDTH), pl.BlockSpec((tm, SM_COLS), lambda i: (i, small_block))]
             + [full(a) for a in ins[2:]])
    if not first:
        ins += [lp["v_up"], lp["v0"], v_first]
        specs += [full(lp["v_up"]), full(lp["v0"]), row(WIDTH)]
    tail = [lp["k_k"], lp["k_a"], lp["r_k"], sel, selt]
    ins += tail
    specs += [full(a) for a in tail]
    dtypes = [BF16, F32] + [BF16] * 6 + ([F32] if first else [])
    kern = functools.partial(_prep_kernel, tm=tm, tiles_per_batch=tp // tm, first=first)
    return pl.pallas_call(
        kern,
        grid=(n // tm,),
        in_specs=specs,
        out_specs=[row(WIDTH)] * len(dtypes),
        out_shape=[jax.ShapeDtypeStruct((n, WIDTH), dt) for dt in dtypes],
        scratch_shapes=[pltpu.VMEM((1, 3 * WIDTH), F32), pltpu.VMEM((1, SM_COLS), F32)],
        compiler_params=_params(("arbitrary",)),
        name="rwkv_prep",
    )(*ins)


def _scan_kernel(r_ref, lw_ref, k_ref, v_ref, kn_ref, a_ref, g_ref, bg_ref, lg_ref, lb_ref,
                 o_ref, s_sc, *, tt):
    ti = pl.program_id(0)

    @pl.when(ti == 0)
    def _():
        s_sc[...] = jnp.zeros_like(s_sc)

    w = SCAN_W
    ri = lax.broadcasted_iota(jnp.int32, (w, w), 0)
    ci = lax.broadcasted_iota(jnp.int32, (w, w), 1)
    same_head = (ri // HEAD_DIM) == (ci // HEAD_DIM)
    ones_bd = same_head.astype(BF16)
    rw = lax.broadcasted_iota(jnp.int32, (CHUNK, w), 0)
    sw = lax.broadcasted_iota(jnp.int32, (CHUNK, w), 1) % HEAD_DIM
    strict = sw < rw
    incl = sw <= rw
    eye = (sw == rw).astype(F32)
    tri_r = lax.broadcasted_iota(jnp.int32, (CHUNK, CHUNK), 0)
    tri_c = lax.broadcasted_iota(jnp.int32, (CHUNK, CHUNK), 1)
    tri = (tri_c <= tri_r).astype(BF16)

    def stack(x):
        return jnp.where(same_head, jnp.tile(x, (SCAN_HEADS, 1)), 0.0).astype(BF16)

    def rows2(x, y):
        return jnp.concatenate([x, y], axis=0).astype(BF16)

    def head_mean(x):
        hi = x.astype(BF16)
        both = jnp.dot(rows2(hi, x - hi.astype(F32)), ones_bd, preferred_element_type=F32)
        return (both[0:CHUNK] + both[CHUNK:2 * CHUNK]) * (1.0 / HEAD_DIM)

    def mm(x, y):
        return jnp.dot(x, y, preferred_element_type=F32)

    def mm_nt(x, y):
        return lax.dot_general(x, y, (((1,), (1,)), ((), ())), preferred_element_type=F32)

    def mm_tn(x, y):
        return lax.dot_general(x, y, (((0,), (0,)), ((), ())), preferred_element_type=F32)

    per_batch = WIDTH // w
    groups = range(r_ref.shape[0] * per_batch)

    def where(gi):
        return gi // per_batch, slice((gi % per_batch) * w, (gi % per_batch + 1) * w)

    def prepare(rows, gi):
        bi, cols = where(gi)
        lw = lw_ref[bi, rows, cols]
        r, k, v, kn, a = (x[bi, rows, cols].astype(F32) for x in (r_ref, k_ref, v_ref, kn_ref, a_ref))
        cum = None
        rest = lw
        for _ in range(3):
            piece = rest.astype(BF16)
            term = jnp.dot(tri, piece, preferred_element_type=F32)
            cum = term if cum is None else cum + term
            rest = rest - piece.astype(F32)
        p_in = jnp.exp(cum)
        p_ex = jnp.exp(cum - lw)
        p_inv = jnp.exp(-cum)
        p_end = p_in[CHUNK - 1:CHUNK, :]
        bt = kn * a * p_inv
        kt = k * p_inv
        return dict(ar=rows2(-kn * p_ex, r * p_in),
                    bk=jnp.concatenate([stack(bt), stack(kt)], axis=0),
                    v=v, vs=stack(v), p_end=p_end, bh_kh=rows2(bt * p_end, kt * p_end))

    def state_free(rows, out):
        g = [prepare(rows, gi) for gi in groups]
        gram = [mm_nt(x["ar"], x["bk"]) for x in g]
        yield
        a_ab = [jnp.where(strict, m[0:CHUNK, 0:w], 0.0) for m in gram]
        a_kv = [rows2(jnp.where(strict, m[0:CHUNK, w:2 * w], 0.0),
                      jnp.where(incl, m[CHUNK:2 * CHUNK, w:2 * w], 0.0)) for m in gram]
        a_rb = [jnp.where(incl, m[CHUNK:2 * CHUNK, 0:w], 0.0).astype(BF16) for m in gram]
        inv = [eye + m for m in a_ab]
        apow = [mm(m.astype(BF16), stack(m)) for m in a_ab]
        yield
        for _ in range(int(math.log2(CHUNK)) - 2):
            both = [mm(rows2(m, t), stack(m)) for m, t in zip(apow, inv)]
            inv = [t + b_[CHUNK:2 * CHUNK] for t, b_ in zip(inv, both)]
            apow = [b_[0:CHUNK] for b_ in both]
            yield
        inv = [t + mm(t.astype(BF16), stack(m)) for t, m in zip(inv, apow)]
        out.update(g=g, a_kv=a_kv, a_rb=a_rb, inv=[t.astype(BF16) for t in inv])

    def state_step(rows, pre, s0, s_out):
        g, a_kv, a_rb, inv = pre["g"], pre["a_kv"], pre["a_rb"], pre["inv"]
        xr = [mm_nt(x["ar"], s.astype(BF16)) for x, s in zip(g, s0)]
        av = [mm(m, x["vs"]) for m, x in zip(a_kv, g)]
        yield
        u = [mm(t, stack(x_[0:CHUNK] + v_[0:CHUNK])) for t, x_, v_ in zip(inv, xr, av)]
        yield
        for gi in groups:
            upd = mm_tn(rows2(u[gi], g[gi]["v"]), g[gi]["bh_kh"])
            s_out.append(s0[gi] * g[gi]["p_end"] + jnp.where(same_head, upd, 0.0))
        y = [x_[CHUNK:2 * CHUNK] + mm(m, stack(u_)) + v_[CHUNK:2 * CHUNK]
             for x_, m, u_, v_ in zip(xr, a_rb, u, av)]
        yield
        yc = [y_ - head_mean(y_) for y_ in y]
        yield
        var = [head_mean(c_ * c_) for c_ in yc]
        yield
        for gi in groups:
            bi, cols = where(gi)
            yn = yc[gi] * lax.rsqrt(var[gi] + GN_EPS) * lg_ref[:, cols] + lb_ref[:, cols]
            o_ref[bi, rows, cols] = (yn * g_ref[bi, rows, cols].astype(F32)
                                     + bg_ref[bi, rows, cols].astype(F32)).astype(o_ref.dtype)

    def interleave(*gens):
        live = list(gens)
        while live:
            for gen in list(live):
                try:
                    next(gen)
                except StopIteration:
                    live.remove(gen)

    n_chunks = tt // CHUNK
    chunk_rows = [pl.ds(c * CHUNK, CHUNK) for c in range(n_chunks)]
    state = [s_sc[gi] for gi in groups]
    pre = {}
    interleave(state_free(chunk_rows[0], pre))
    for c in range(n_chunks):
        nxt, new_state = {}, []
        gens = [state_step(chunk_rows[c], pre, state, new_state)]
        if c + 1 < n_chunks:
            gens.insert(0, state_free(chunk_rows[c + 1], nxt))
        interleave(*gens)
        pre, state = nxt, new_state
    for gi in groups:
        s_sc[gi] = state[gi]


def _rwkv_scan(r, lw, k, v, kn, a, g, bg, lnx_g, lnx_b, b, tp):
    n = r.shape[0]
    tt = _tile(tp, 320)
    spec = pl.BlockSpec((b, tt, WIDTH), lambda ti: (0, ti, 0))
    vec = pl.BlockSpec((1, WIDTH), lambda ti: (0, 0))
    kern = functools.partial(_scan_kernel, tt=tt)
    out = pl.pallas_call(
        kern,
        grid=(tp // tt,),
        in_specs=[spec] * 8 + [vec, vec],
        out_specs=spec,
        out_shape=jax.ShapeDtypeStruct((b, tp, WIDTH), BF16),
        scratch_shapes=[pltpu.VMEM((b * (WIDTH // SCAN_W), SCAN_W, SCAN_W), F32)],
        compiler_params=_params(("arbitrary",)),
        name="rwkv_scan",
    )(*(x.reshape(b, tp, WIDTH) for x in (r, lw, k, v, kn, a, g, bg)), lnx_g, lnx_b)
    return out.reshape(n, WIDTH)


def _outproj_kernel(fox_ref, rwkv_ref, h_ref, wt_ref, wb_ref, g_ref, b_ref, of_ref, ob_ref):
    mixed = (jnp.dot(fox_ref[...], wt_ref[...], preferred_element_type=F32)
             + jnp.dot(rwkv_ref[...], wb_ref[...], preferred_element_type=F32))
    y = _layer_norm(ALPHA * h_ref[...] + mixed, g_ref[...], b_ref[...])
    of_ref[...] = y
    ob_ref[...] = y.astype(BF16)


def _outproj_ln(fox, rwkv, h, w_out, layer, g, b, tp):
    n, d = h.shape
    tm = _tile(tp, 320)
    half = pl.BlockSpec((tm, WIDTH), lambda i: (i, 0))
    row = pl.BlockSpec((tm, d), lambda i: (i, 0))
    vec = pl.BlockSpec((1, d), lambda i: (0, 0))
    return pl.pallas_call(
        _outproj_kernel,
        grid=(n // tm,),
        in_specs=[half, half, row,
                  pl.BlockSpec((None, WIDTH, d), lambda i: (layer, 0, 0)),
                  pl.BlockSpec((None, WIDTH, d), lambda i: (layer, 1, 0)),
                  vec, vec],
        out_specs=[row, row],
        out_shape=[jax.ShapeDtypeStruct((n, d), F32), jax.ShapeDtypeStruct((n, d), BF16)],
        compiler_params=_params(("parallel",)),
        name="outproj_ln",
    )(fox, rwkv, h, w_out, w_out, g, b)


def _ffn_kernel(xb_ref, xf_ref, w1_ref, w2_ref, g_ref, b_ref, of_ref, ob_ref, acc_sc):
    j = pl.program_id(1)

    @pl.when(j == 0)
    def _():
        acc_sc[...] = jnp.zeros_like(acc_sc)

    u = jnp.maximum(jnp.dot(xb_ref[...], w1_ref[...], preferred_element_type=F32), 0.0)
    acc_sc[...] += jnp.dot((u * u).astype(BF16), w2_ref[...], preferred_element_type=F32)

    @pl.when(j == pl.num_programs(1) - 1)
    def _():
        y = _layer_norm(ALPHA * xf_ref[...] + acc_sc[...], g_ref[...], b_ref[...])
        of_ref[...] = y
        ob_ref[...] = y.astype(BF16)


def _ffn_ln(xb, xf, w1, w2, layer, g, b, tp):
    n, d = xf.shape
    tm = _tile(tp, 640)
    tf = 1024
    row = pl.BlockSpec((tm, d), lambda i, j: (i, 0))
    vec = pl.BlockSpec((1, d), lambda i, j: (0, 0))
    return pl.pallas_call(
        _ffn_kernel,
        grid=(n // tm, D_FF // tf),
        in_specs=[row, row,
                  pl.BlockSpec((None, d, tf), lambda i, j: (layer, 0, j)),
                  pl.BlockSpec((None, tf, d), lambda i, j: (layer, j, 0)),
                  vec, vec],
        out_specs=[pl.BlockSpec((tm, d), lambda i, j: (i, 0), pipeline_mode=pl.Buffered(1)),
                   pl.BlockSpec((tm, d), lambda i, j: (i, 0), pipeline_mode=pl.Buffered(1))],
        out_shape=[jax.ShapeDtypeStruct((n, d), F32), jax.ShapeDtypeStruct((n, d), BF16)],
        scratch_shapes=[pltpu.VMEM((tm, d), F32)],
        compiler_params=_params(("parallel", "arbitrary")),
        name="ffn_ln",
    )(xb, xf, w1, w2, g, b)


_RS = 3 * WIDTH + HEADS
_O_W = _RS + 3 * WIDTH
_O_A = _O_W + DECAY_RANK
_O_G = _O_A + AAA_RANK
_O_V = _O_G + GATE_RANK


def _win_layout_kernel(w_ref, wf_ref, wr_ref, *, first):
    w = w_ref[...]
    lane = lax.broadcasted_iota(jnp.int32, (1, 3 * WIDTH), 1)
    col_scale = jnp.where(lane < WIDTH, HEAD_DIM ** -0.5 * LOG2E, 1.0)
    wf_ref[...] = (w[:, 0:3 * WIDTH] * col_scale).astype(BF16)
    s0 = 3 * WIDTH
    wr_ref[:, 0:s0] = w[:, _RS:_O_W].astype(BF16)
    wr_ref[:, s0:s0 + SM_COLS] = jnp.zeros((w.shape[0], SM_COLS), BF16)
    wr_ref[:, s0 + SM_F:s0 + SM_F + HEADS] = w[:, 3 * WIDTH:_RS].astype(BF16)
    wr_ref[:, s0 + SM_W:s0 + SM_W + DECAY_RANK] = w[:, _O_W:_O_A].astype(BF16)
    wr_ref[:, s0 + SM_A:s0 + SM_A + AAA_RANK] = w[:, _O_A:_O_G].astype(BF16)
    wr_ref[:, s0 + SM_G:s0 + SM_G + GATE_RANK] = w[:, _O_G:_O_V].astype(BF16)
    if not first:
        wr_ref[:, s0 + SM_V:s0 + SM_V + MV_RANK] = w[:, _O_V:_O_V + MV_RANK].astype(BF16)


def _win_layout(w_in, layer):
    first = layer is None
    d, cols = w_in.shape[-2:]
    tr = 256
    if first:
        in_spec = pl.BlockSpec((tr, cols), lambda i: (i, 0))
    else:
        in_spec = pl.BlockSpec((None, tr, cols), lambda i: (layer, i, 0))
    out = lambda c: pl.BlockSpec((tr, c), lambda i: (i, 0))
    return pl.pallas_call(
        functools.partial(_win_layout_kernel, first=first),
        grid=(d // tr,),
        in_specs=[in_spec],
        out_specs=[out(3 * WIDTH), out(3 * WIDTH + SM_COLS)],
        out_shape=[jax.ShapeDtypeStruct((d, 3 * WIDTH), BF16),
                   jax.ShapeDtypeStruct((d, 3 * WIDTH + SM_COLS), BF16)],
        compiler_params=_params(("parallel",)),
        name="win_layout",
    )(w_in)


def _pad_cols(a, width):
    return jnp.pad(a, ((0, 0), (0, width - a.shape[1])))


def _pad_rows(a, height):
    return jnp.pad(a, ((0, height - a.shape[0]), (0, 0)))


def _layer_params(w_in, layer, mu, fox_fb, fox_out_g, w_up, w0, a_up, a0, g_up, v_up, v0,
                  k_k, k_a, r_k, lnx_g, lnx_b, ln1_g, ln1_b, ln2_g, ln2_b):
    first = v_up is None
    w_fox, w_rwkv = _win_layout(w_in, layer)
    o_w = 3 * WIDTH
    o_a = o_w + DECAY_RANK
    o_g = o_a + AAA_RANK
    o_v = o_g + GATE_RANK
    mus = [(jnp.zeros((1, HEADS), F32), SM_W - SM_F), (mu[None, o_w:o_a], SM_A - SM_W),
           (mu[None, o_a:o_g], SM_G - SM_A), (mu[None, o_g:o_v], SM_V - SM_G),
           (mu[None, o_v:], SM_COLS - SM_V)]
    mu_small = jnp.concatenate([_pad_cols(a, wd) for a, wd in mus], axis=1)
    row = lambda a: a.reshape(1, -1)
    lp = dict(
        w_fox=w_fox, w_rwkv=w_rwkv,
        mu_big=mu[None, 0:o_w], mu_small=mu_small,
        fb=_pad_cols(fox_fb[None, :], LANES),
        gains=fox_out_g.reshape(HEADS // 2, 1, 2 * HEAD_DIM),
        w_up=_pad_rows(w_up, SM_A - SM_W).astype(BF16), w0=row(w0),
        a_up=_pad_rows(a_up, SM_G - SM_A).astype(BF16), a0=row(a0),
        g_up=_pad_rows(g_up, SM_V - SM_G).astype(BF16),
        k_k=row(k_k), k_a=row(k_a), r_k=row(r_k), lnx_g=row(lnx_g), lnx_b=row(lnx_b),
        ln1_g=row(ln1_g), ln1_b=row(ln1_b), ln2_g=row(ln2_g), ln2_b=row(ln2_b),
    )
    if not first:
        lp["v_up"] = _pad_rows(v_up, SM_COLS - SM_V).astype(BF16)
        lp["v0"] = row(v0)
    return lp


def _layer(hf, hb, lp, layer, w_out, w_ff1, w_ff2, sel, selt, v_first, b, tp):
    qkv = _matmul(hb, lp["w_fox"], BF16, "inproj_fox")
    proj = _matmul(hb, lp["w_rwkv"], F32, "inproj_rwkv")
    qp, kp, vt = _fox_prep(qkv, proj, lp["fb"], b, tp)
    fox = _fox_attention(qp, kp, vt, lp["gains"], b, tp)
    outs = _rwkv_prep(proj, lp, sel, selt, v_first, tp)
    r, lw, k, v, kn, a, g, bg = outs[:8]
    if v_first is None:
        v_first = outs[8]
    rwkv = _rwkv_scan(r, lw, k, v, kn, a, g, bg, lp["lnx_g"], lp["lnx_b"], b, tp)
    xf, xb = _outproj_ln(fox, rwkv, hf, w_out, layer, lp["ln1_g"], lp["ln1_b"], tp)
    hf, hb = _ffn_ln(xb, xf, w_ff1, w_ff2, layer, lp["ln2_g"], lp["ln2_b"], tp)
    return hf, hb, v_first


def kernel(x, meta, ln_in_g, ln_in_b, w_in_first, w_in_rest, mu_first, mu_rest, fox_fb, fox_out_g, w_up, w0, a_up, a0, g_up, v_up, v0, k_k, k_a, r_k, lnx_g, lnx_b, w_out, ln1_g, ln1_b, w_ff1, w_ff2, ln2_g, ln2_b):
    b, seq, d = x.shape
    assert d == D_MODEL
    tp = FRONT_PAD + N_META + seq
    assert tp % LANES == 0
    head = jnp.concatenate([jnp.zeros((FRONT_PAD, d), x.dtype), meta.astype(x.dtype)], axis=0)
    h0 = jnp.concatenate([jnp.broadcast_to(head[None], (b, X_START, d)), x], axis=1).reshape(b * tp, d)
    hf, hb = _ln_rows(h0, ln_in_g, ln_in_b, tp)

    lane_head = jnp.arange(WIDTH, dtype=jnp.int32) // HEAD_DIM
    sel = (lane_head[:, None] == jnp.arange(LANES, dtype=jnp.int32)[None, :]).astype(BF16)
    selt = sel.T

    w_out_b, w_ff1_b, w_ff2_b = w_out.astype(BF16), w_ff1.astype(BF16), w_ff2.astype(BF16)

    v_first = None
    for l in range(DEPTH):
        first = l == 0
        lp = _layer_params(
            w_in_first if first else w_in_rest, None if first else l - 1,
            mu_first if first else mu_rest[l - 1],
            fox_fb[l], fox_out_g[l], w_up[l], w0[l], a_up[l], a0[l], g_up[l],
            None if first else v_up[l - 1], None if first else v0[l - 1],
            k_k[l], k_a[l], r_k[l], lnx_g[l], lnx_b[l],
            ln1_g[l], ln1_b[l], ln2_g[l], ln2_b[l])
        hf, hb, v_first = _layer(hf, hb, lp, l, w_out_b, w_ff1_b, w_ff2_b, sel, selt, v_first, b, tp)
    return hf.reshape(b, tp, d)[:, X_START:, :]
```
